```python
import jax, jax.numpy as jnp
from jax import lax
import numpy as np

D_MODEL = 1024
BATCH = 4
SEQ = 4096
DEPTH = 2
DEC_BATCH = 2
DEC_SEQ = 8192
PAST_LEN = 128

N_MIXERS = 2
D_FF = 4096
CONV_KERNEL = 31
HEAD_DIM = 64
N_HEADS = D_MODEL // HEAD_DIM
N_KV_HEADS = 4
GROUP = N_HEADS // N_KV_HEADS
WINDOW = 128
BLOCK = 128
ROPE_THETA = 10000.0
EPS = 1e-6
N_SUBLAYER_NORMS = 6
N_CONV_LAYERS = (DEPTH + 1) // 2
N_ATTN_LAYERS = DEPTH // 2
QKV_DIM = (N_HEADS + 2 * N_KV_HEADS) * HEAD_DIM
NEG_INF = -1e30

kernel_name = "hybrid_conformer_swa_encoder"


def rms_norm(x, g):
    xf = x.astype(jnp.float32)
    y = xf * lax.rsqrt(jnp.mean(xf * xf, axis=-1, keepdims=True) + EPS)
    return (y * g.astype(jnp.float32)).astype(x.dtype)


def layer_norm(x, g, b):
    xf = x.astype(jnp.float32)
    mu = jnp.mean(xf, axis=-1, keepdims=True)
    var = jnp.mean(jnp.square(xf - mu), axis=-1, keepdims=True)
    y = (xf - mu) * lax.rsqrt(var + EPS)
    return (y * g.astype(jnp.float32) + b.astype(jnp.float32)).astype(x.dtype)


def swiglu(x, w_gate, w_up, w_down):
    return (jax.nn.silu(x @ w_gate) * (x @ w_up)) @ w_down


def conformer_conv(x, w_pw1, b_pw1, w_dw, b_dw, ln_g, ln_b, w_pw2, b_pw2):
    h = x @ w_pw1 + b_pw1
    a, gate = jnp.split(h, 2, axis=-1)
    h = a * jax.nn.sigmoid(gate)
    h = lax.conv_general_dilated(
        h, w_dw[:, None, :], window_strides=(1,),
        padding=[(CONV_KERNEL // 2, CONV_KERNEL // 2)],
        dimension_numbers=('NWC', 'WIO', 'NWC'),
        feature_group_count=D_MODEL) + b_dw
    h = jax.nn.silu(layer_norm(h, ln_g, ln_b))
    return h @ w_pw2 + b_pw2


def rope(x, pos):
    half = HEAD_DIM // 2
    inv = ROPE_THETA ** (-jnp.arange(half, dtype=jnp.float32) / half)
    ang = pos.astype(jnp.float32)[:, None] * inv[None, :]
    cos = jnp.cos(ang)[None, :, None, :]
    sin = jnp.sin(ang)[None, :, None, :]
    xf = x.astype(jnp.float32)
    x1, x2 = xf[..., :half], xf[..., half:]
    out = jnp.concatenate([x1 * cos - x2 * sin, x2 * cos + x1 * sin], axis=-1)
    return out.astype(x.dtype)


def windowed_gqa(x, w_qkv, w_o, sink):
    B, S, _ = x.shape
    nb = S // BLOCK
    qkv = x @ w_qkv
    q = qkv[..., :N_HEADS * HEAD_DIM].reshape(B, S, N_HEADS, HEAD_DIM)
    k = qkv[..., N_HEADS * HEAD_DIM:(N_HEADS + N_KV_HEADS) * HEAD_DIM].reshape(B, S, N_KV_HEADS, HEAD_DIM)
    v = qkv[..., (N_HEADS + N_KV_HEADS) * HEAD_DIM:].reshape(B, S, N_KV_HEADS, HEAD_DIM)
    pos = jnp.arange(S)
    q = rope(q, pos)
    k = rope(k, pos)

    def band(t):
        tp = jnp.pad(t, ((0, 0), (BLOCK, BLOCK), (0, 0), (0, 0)))
        tp = tp.reshape(B, nb + 2, BLOCK, N_KV_HEADS, HEAD_DIM)
        return jnp.concatenate([tp[:, :-2], tp[:, 1:-1], tp[:, 2:]], axis=2)

    kb, vb = band(k), band(v)
    qb = q.reshape(B, nb, BLOCK, N_KV_HEADS, GROUP, HEAD_DIM)
    s = jnp.einsum('bnqkgd,bnjkd->bnkgqj', qb, kb,
                   preferred_element_type=jnp.float32) * (HEAD_DIM ** -0.5)
    qi = jnp.arange(nb)[:, None, None] * BLOCK + jnp.arange(BLOCK)[None, :, None]
    kj = (jnp.arange(nb)[:, None, None] - 1) * BLOCK + jnp.arange(3 * BLOCK)[None, None, :]
    mask = (jnp.abs(kj - qi) <= WINDOW) & (kj >= 0) & (kj < S)
    s = jnp.where(mask[None, :, None, None], s, NEG_INF)
    sk = sink.astype(jnp.float32).reshape(N_KV_HEADS, GROUP)[None, None, :, :, None, None]
    m = jnp.maximum(jnp.max(s, axis=-1, keepdims=True), sk)
    e = jnp.exp(s - m)
    p = e / (jnp.sum(e, axis=-1, keepdims=True) + jnp.exp(sk - m))
    o = jnp.einsum('bnkgqj,bnjkd->bnqkgd', p.astype(vb.dtype), vb)
    return o.reshape(B, S, N_HEADS * HEAD_DIM) @ w_o


def trunk(x, norm_g, ffn_w_gate, ffn_w_up, ffn_w_down,
          conv_w_pw1, conv_b_pw1, conv_w_dw, conv_b_dw, conv_ln_g, conv_ln_b,
          conv_w_pw2, conv_b_pw2, attn_w_qkv, attn_w_o, attn_sink):
    for i in range(DEPTH):
        g = norm_g[i]
        h = swiglu(rms_norm(x, g[0]), ffn_w_gate[i, 0], ffn_w_up[i, 0], ffn_w_down[i, 0])
        x = x + 0.5 * rms_norm(h, g[1])
        hn = rms_norm(x, g[2])
        j = i // N_MIXERS
        if i % N_MIXERS == 0:
            h = conformer_conv(hn, conv_w_pw1[j], conv_b_pw1[j], conv_w_dw[j], conv_b_dw[j],
                               conv_ln_g[j], conv_ln_b[j], conv_w_pw2[j], conv_b_pw2[j])
        else:
            h = windowed_gqa(hn, attn_w_qkv[j], attn_w_o[j], attn_sink[j])
        x = x + rms_norm(h, g[3])
        h = swiglu(rms_norm(x, g[4]), ffn_w_gate[i, 1], ffn_w_up[i, 1], ffn_w_down[i, 1])
        x = x + 0.5 * rms_norm(h, g[5])
    return x


def setup_inputs(seed: int = 0) -> dict:
    key = jax.random.key(seed)
    ks = jax.random.split(key, 20)
    f32 = jnp.float32
    nrm = lambda k, shape, scale: jax.random.normal(k, shape, f32) * scale
    return {
        "x_prompt": nrm(ks[0], (BATCH, SEQ, D_MODEL), 1.0),
        "x_sample": nrm(ks[1], (DEC_BATCH, DEC_SEQ, D_MODEL), 1.0),
        "norm_g": 1.0 + nrm(ks[2], (DEPTH, N_SUBLAYER_NORMS, D_MODEL), 0.02),
        "ffn_w_gate": nrm(ks[3], (DEPTH, 2, D_MODEL, D_FF), D_MODEL ** -0.5),
        "ffn_w_up": nrm(ks[4], (DEPTH, 2, D_MODEL, D_FF), D_MODEL ** -0.5),
        "ffn_w_down": nrm(ks[5], (DEPTH, 2, D_FF, D_MODEL), D_FF ** -0.5),
        "conv_w_pw1": nrm(ks[6], (N_CONV_LAYERS, D_MODEL, 2 * D_MODEL), D_MODEL ** -0.5),
        "conv_b_pw1": nrm(ks[7], (N_CONV_LAYERS, 2 * D_MODEL), 0.01),
        "conv_w_dw": nrm(ks[8], (N_CONV_LAYERS, CONV_KERNEL, D_MODEL), CONV_KERNEL ** -0.5),
        "conv_b_dw": nrm(ks[9], (N_CONV_LAYERS, D_MODEL), 0.01),
        "conv_ln_g": 1.0 + nrm(ks[10], (N_CONV_LAYERS, D_MODEL), 0.02),
        "conv_ln_b": nrm(ks[11], (N_CONV_LAYERS, D_MODEL), 0.01),
        "conv_w_pw2": nrm(ks[12], (N_CONV_LAYERS, D_MODEL, D_MODEL), D_MODEL ** -0.5),
        "conv_b_pw2": nrm(ks[13], (N_CONV_LAYERS, D_MODEL), 0.01),
        "attn_w_qkv": nrm(ks[14], (N_ATTN_LAYERS, D_MODEL, QKV_DIM), D_MODEL ** -0.5),
        "attn_w_o": nrm(ks[15], (N_ATTN_LAYERS, N_HEADS * HEAD_DIM, D_MODEL), (N_HEADS * HEAD_DIM) ** -0.5),
        "attn_sink": nrm(ks[16], (N_ATTN_LAYERS, N_HEADS), 0.5),
    }


def reference(x_prompt, x_sample, norm_g, ffn_w_gate, ffn_w_up, ffn_w_down,
              conv_w_pw1, conv_b_pw1, conv_w_dw, conv_b_dw, conv_ln_g, conv_ln_b,
              conv_w_pw2, conv_b_pw2, attn_w_qkv, attn_w_o, attn_sink):
    y_prompt = trunk(x_prompt, norm_g, ffn_w_gate, ffn_w_up, ffn_w_down,
                     conv_w_pw1, conv_b_pw1, conv_w_dw, conv_b_dw, conv_ln_g, conv_ln_b,
                     conv_w_pw2, conv_b_pw2, attn_w_qkv, attn_w_o, attn_sink)
    y_sample = trunk(x_sample, norm_g, ffn_w_gate, ffn_w_up, ffn_w_down,
                     conv_w_pw1, conv_b_pw1, conv_w_dw, conv_b_dw, conv_ln_g, conv_ln_b,
                     conv_w_pw2, conv_b_pw2, attn_w_qkv, attn_w_o, attn_sink)
    return (y_prompt, y_sample)
```

```python
import functools

import jax
import jax.numpy as jnp
from jax import lax
from jax.experimental import pallas as pl
from jax.experimental.pallas import tpu as pltpu

F32 = jnp.float32
BF16 = jnp.bfloat16

D_MODEL = 1024
D_FF = 4096
CONV_KERNEL = 31
CONV_HALF = CONV_KERNEL // 2
HEAD_DIM = 64
N_HEADS = 16
N_KV_HEADS = 4
GROUP = N_HEADS // N_KV_HEADS
Q_DIM = N_HEADS * HEAD_DIM
KV_DIM = N_KV_HEADS * HEAD_DIM
WINDOW = 128
BLOCK = 128
ROPE_THETA = 10000.0
EPS = 1e-6
NEG_INF = -1e30

LANES = 128
SUBLANES = 8
HALO_ROWS = 16

FFN_TM = 1024
FFN_TF = 512
GLU_TM = 512
CONV_TM = 512
CONV_ROWS = 64
QKV_TM = 512
ATT_TQ = 512

VMEM_TEMP_BYTES = 16 * 1024 * 1024


def _vmem_limit(pipelined_bytes, scratch_bytes):
    return int(2 * pipelined_bytes + scratch_bytes + VMEM_TEMP_BYTES)


def _nbytes(shape, dtype):
    n = 1
    for s in shape:
        n *= s
    return n * jnp.dtype(dtype).itemsize


def _rms(x, g):
    ms = jnp.mean(x * x, axis=-1, keepdims=True)
    return x * lax.rsqrt(ms + EPS) * g


def _sigmoid(x):
    return 1.0 / (1.0 + jnp.exp(-x))


def _ffn_body(x_ref, gpre_ref, gpost_ref, wg_ref, wu_ref, wd_ref, o_ref, xn_ref, acc_ref):
    j = pl.program_id(1)

    @pl.when(j == 0)
    def _():
        xn_ref[...] = _rms(x_ref[...], gpre_ref[...]).astype(BF16)
        acc_ref[...] = jnp.zeros_like(acc_ref)

    xn = xn_ref[...]
    gate = jnp.dot(xn, wg_ref[...], preferred_element_type=F32)
    up = jnp.dot(xn, wu_ref[...], preferred_element_type=F32)
    h = (gate * _sigmoid(gate) * up).astype(BF16)
    acc_ref[...] += jnp.dot(h, wd_ref[...], preferred_element_type=F32)

    @pl.when(j == pl.num_programs(1) - 1)
    def _():
        o_ref[...] = x_ref[...] + 0.5 * _rms(acc_ref[...], gpost_ref[...])


def _ffn(x, g_pre, g_post, wg, wu, wd):
    t = x.shape[0]
    tm, tf = FFN_TM, FFN_TF
    blocks = (_nbytes((tm, D_MODEL), F32) * 2 + 2 * _nbytes((D_MODEL, tf), BF16)
              + _nbytes((tf, D_MODEL), BF16) + 2 * _nbytes((1, D_MODEL), F32))
    scratch = _nbytes((tm, D_MODEL), BF16) + _nbytes((tm, D_MODEL), F32)
    return pl.pallas_call(
        _ffn_body,
        out_shape=jax.ShapeDtypeStruct((t, D_MODEL), F32),
        grid=(t // tm, D_FF // tf),
        in_specs=[
            pl.BlockSpec((tm, D_MODEL), lambda i, j: (i, 0)),
            pl.BlockSpec((1, D_MODEL), lambda i, j: (0, 0)),
            pl.BlockSpec((1, D_MODEL), lambda i, j: (0, 0)),
            pl.BlockSpec((D_MODEL, tf), lambda i, j: (0, j)),
            pl.BlockSpec((D_MODEL, tf), lambda i, j: (0, j)),
            pl.BlockSpec((tf, D_MODEL), lambda i, j: (j, 0)),
        ],
        out_specs=pl.BlockSpec((tm, D_MODEL), lambda i, j: (i, 0)),
        scratch_shapes=[pltpu.VMEM((tm, D_MODEL), BF16), pltpu.VMEM((tm, D_MODEL), F32)],
        compiler_params=pltpu.CompilerParams(
            dimension_semantics=("parallel", "arbitrary"),
            vmem_limit_bytes=_vmem_limit(blocks, scratch)),
        name="ffn",
    )(x, g_pre, g_post, wg, wu, wd)


def _glu_body(x_ref, g_ref, w_ref, b_ref, o_ref):
    hn = _rms(x_ref[...], g_ref[...]).astype(BF16)
    h = jnp.dot(hn, w_ref[...], preferred_element_type=F32) + b_ref[...]
    o_ref[...] = h[:, :D_MODEL] * _sigmoid(h[:, D_MODEL:])


def _glu(x, g, w_pw1, b_pw1):
    t = x.shape[0]
    tm = GLU_TM
    blocks = (2 * _nbytes((tm, D_MODEL), F32) + _nbytes((D_MODEL, 2 * D_MODEL), BF16)
              + 3 * _nbytes((1, D_MODEL), F32))
    return pl.pallas_call(
        _glu_body,
        out_shape=jax.ShapeDtypeStruct((t, D_MODEL), F32),
        grid=(t // tm,),
        in_specs=[
            pl.BlockSpec((tm, D_MODEL), lambda i: (i, 0)),
            pl.BlockSpec((1, D_MODEL), lambda i: (0, 0)),
            pl.BlockSpec((D_MODEL, 2 * D_MODEL), lambda i: (0, 0)),
            pl.BlockSpec((1, 2 * D_MODEL), lambda i: (0, 0)),
        ],
        out_specs=pl.BlockSpec((tm, D_MODEL), lambda i: (i, 0)),
        compiler_params=pltpu.CompilerParams(
            dimension_semantics=("parallel",),
            vmem_limit_bytes=_vmem_limit(blocks, 0)),
        name="glu",
    )(x, g, w_pw1, b_pw1)


def _convmix_body(x_ref, h_ref, hprev_ref, hnext_ref, wdw_ref, bdw_ref, lng_ref, lnb_ref,
                  w2_ref, b2_ref, g_ref, o_ref, hp_ref, c_ref):
    i = pl.program_id(1)
    tm = h_ref.shape[1]
    hp_ref[0:HALO_ROWS] = jnp.where(i > 0, hprev_ref[0], 0.0)
    hp_ref[HALO_ROWS:HALO_ROWS + tm] = h_ref[0]
    hp_ref[HALO_ROWS + tm:] = jnp.where(i < pl.num_programs(1) - 1, hnext_ref[0], 0.0)

    base = HALO_ROWS - CONV_HALF
    win = CONV_ROWS + 2 * HALO_ROWS
    for c in range(D_MODEL // LANES):
        cols = slice(c * LANES, (c + 1) * LANES)

        def rows(r, carry, cols=cols):
            r0 = pl.multiple_of(r * CONV_ROWS, CONV_ROWS)
            v = hp_ref[pl.ds(r0, win), cols]
            acc = None
            for shift in range(SUBLANES):
                vs = v if shift == 0 else pltpu.roll(v, win - shift, 0)
                for tile in range(2 * HALO_ROWS // SUBLANES):
                    k = tile * SUBLANES + shift - base
                    if 0 <= k < CONV_KERNEL:
                        term = vs[tile * SUBLANES:tile * SUBLANES + CONV_ROWS] * wdw_ref[k:k + 1, cols]
                        acc = term if acc is None else acc + term
            c_ref[pl.ds(r0, CONV_ROWS), cols] = acc
            return carry

        lax.fori_loop(0, tm // CONV_ROWS, rows, 0)

    y = c_ref[...] + bdw_ref[...]
    mu = jnp.mean(y, axis=-1, keepdims=True)
    yc = y - mu
    var = jnp.mean(yc * yc, axis=-1, keepdims=True)
    y = yc * lax.rsqrt(var + EPS) * lng_ref[...] + lnb_ref[...]
    y = (y * _sigmoid(y)).astype(BF16)
    h = jnp.dot(y, w2_ref[...], preferred_element_type=F32) + b2_ref[...]
    o_ref[0] = x_ref[0] + _rms(h, g_ref[...])


def _convmix(x, h, w_dw, b_dw, ln_g, ln_b, w_pw2, b_pw2, g):
    b, s, _ = x.shape
    tm = CONV_TM
    halo_per_tile = tm // HALO_ROWS
    n_halo = s // HALO_ROWS
    blocks = (3 * _nbytes((tm, D_MODEL), F32) + 2 * _nbytes((HALO_ROWS, D_MODEL), F32)
              + _nbytes((CONV_KERNEL, D_MODEL), F32) + _nbytes((D_MODEL, D_MODEL), BF16)
              + 6 * _nbytes((1, D_MODEL), F32))
    scratch = _nbytes((tm + 2 * HALO_ROWS, D_MODEL), F32) + _nbytes((tm, D_MODEL), F32)
    vec = pl.BlockSpec((1, D_MODEL), lambda bi, i: (0, 0))
    return pl.pallas_call(
        _convmix_body,
        out_shape=jax.ShapeDtypeStruct((b, s, D_MODEL), F32),
        grid=(b, s // tm),
        in_specs=[
            pl.BlockSpec((1, tm, D_MODEL), lambda bi, i: (bi, i, 0)),
            pl.BlockSpec((1, tm, D_MODEL), lambda bi, i: (bi, i, 0)),
            pl.BlockSpec((1, HALO_ROWS, D_MODEL),
                         lambda bi, i: (bi, jnp.maximum(i * halo_per_tile - 1, 0), 0)),
            pl.BlockSpec((1, HALO_ROWS, D_MODEL),
                         lambda bi, i: (bi, jnp.minimum((i + 1) * halo_per_tile, n_halo - 1), 0)),
            pl.BlockSpec((CONV_KERNEL, D_MODEL), lambda bi, i: (0, 0)),
            vec, vec, vec,
            pl.BlockSpec((D_MODEL, D_MODEL), lambda bi, i: (0, 0)),
            vec, vec,
        ],
        out_specs=pl.BlockSpec((1, tm, D_MODEL), lambda bi, i: (bi, i, 0)),
        scratch_shapes=[pltpu.VMEM((tm + 2 * HALO_ROWS, D_MODEL), F32), pltpu.VMEM((tm, D_MODEL), F32)],
        compiler_params=pltpu.CompilerParams(
            dimension_semantics=("parallel", "arbitrary"),
            vmem_limit_bytes=_vmem_limit(blocks, scratch)),
        name="convmix",
    )(x, h, h, h, w_dw, b_dw, ln_g, ln_b, w_pw2, b_pw2, g)


def _rope_tables(s):
    half = HEAD_DIM // 2
    inv = ROPE_THETA ** (-jnp.arange(half, dtype=F32) / half)
    ang = jnp.arange(s).astype(F32)[:, None] * inv[None, :]
    cos, sin, zero = jnp.cos(ang), jnp.sin(ang), jnp.zeros_like(ang)
    reps = LANES // HEAD_DIM
    cos_t = jnp.tile(jnp.concatenate([cos, cos], axis=-1), (1, reps))
    sin_lo = jnp.tile(jnp.concatenate([-sin, zero], axis=-1), (1, reps))
    sin_hi = jnp.tile(jnp.concatenate([zero, sin], axis=-1), (1, reps))
    return cos_t, sin_lo, sin_hi


def _qkv_body(x_ref, g_ref, w_ref, cos_ref, slo_ref, shi_ref, q_ref, k_ref, v_ref):
    hn = _rms(x_ref[...], g_ref[...]).astype(BF16)
    qkv = jnp.dot(hn, w_ref[...], preferred_element_type=F32)
    cos, slo, shi = cos_ref[...], slo_ref[...], shi_ref[...]
    half = HEAD_DIM // 2

    def rope(xc):
        return xc * cos + pltpu.roll(xc, LANES - half, 1) * slo + pltpu.roll(xc, half, 1) * shi

    for c in range(Q_DIM // LANES):
        cols = slice(c * LANES, (c + 1) * LANES)
        q_ref[:, cols] = rope(qkv[:, cols]).astype(BF16)
    for c in range(KV_DIM // LANES):
        cols = slice(c * LANES, (c + 1) * LANES)
        k_ref[:, cols] = rope(qkv[:, Q_DIM + c * LANES:Q_DIM + (c + 1) * LANES]).astype(BF16)
    v_ref[...] = qkv[:, Q_DIM + KV_DIM:].astype(BF16)


def _qkv(x, g, w_qkv, tables, s):
    t = x.shape[0]
    tm = QKV_TM
    tiles_per_seq = s // tm
    qkv_dim = Q_DIM + 2 * KV_DIM
    blocks = (_nbytes((tm, D_MODEL), F32) + _nbytes((D_MODEL, qkv_dim), BF16) + 3 * _nbytes((tm, LANES), F32)
              + _nbytes((tm, qkv_dim), BF16) + _nbytes((1, D_MODEL), F32))
    tab = pl.BlockSpec((tm, LANES), lambda i: (i % tiles_per_seq, 0))
    return pl.pallas_call(
        _qkv_body,
        out_shape=(jax.ShapeDtypeStruct((t, Q_DIM), BF16), jax.ShapeDtypeStruct((t, KV_DIM), BF16),
                   jax.ShapeDtypeStruct((t, KV_DIM), BF16)),
        grid=(t // tm,),
        in_specs=[
            pl.BlockSpec((tm, D_MODEL), lambda i: (i, 0)),
            pl.BlockSpec((1, D_MODEL), lambda i: (0, 0)),
            pl.BlockSpec((D_MODEL, qkv_dim), lambda i: (0, 0)),
            tab, tab, tab,
        ],
        out_specs=(pl.BlockSpec((tm, Q_DIM), lambda i: (i, 0)), pl.BlockSpec((tm, KV_DIM), lambda i: (i, 0)),
                   pl.BlockSpec((tm, KV_DIM), lambda i: (i, 0))),
        compiler_params=pltpu.CompilerParams(
            dimension_semantics=("parallel",),
            vmem_limit_bytes=_vmem_limit(blocks, 0)),
        name="qkv",
    )(x, g, w_qkv, *tables)


def _attn_body(sink_ref, x_ref, q_ref, kp_ref, k_ref, kn_ref, vp_ref, v_ref, vn_ref, wo_ref, g_ref,
               o_ref, kx_ref, vx_ref, oh_ref):
    i = pl.program_id(1)
    tq = q_ref.shape[1]
    blocks_per_tile = tq // BLOCK
    n_blocks = pl.num_programs(1) * blocks_per_tile
    kx_ref[0:BLOCK] = kp_ref[0]
    kx_ref[BLOCK:BLOCK + tq] = k_ref[0]
    kx_ref[BLOCK + tq:] = kn_ref[0]
    vx_ref[0:BLOCK] = vp_ref[0]
    vx_ref[BLOCK:BLOCK + tq] = v_ref[0]
    vx_ref[BLOCK + tq:] = vn_ref[0]

    r_idx = lax.broadcasted_iota(jnp.int32, (BLOCK, 3 * BLOCK), 0)
    c_idx = lax.broadcasted_iota(jnp.int32, (BLOCK, 3 * BLOCK), 1)
    band = (c_idx >= r_idx + (BLOCK - WINDOW)) & (c_idx <= r_idx + (BLOCK + WINDOW))

    def block(n, carry):
        gb = i * blocks_per_tile + n
        lo = jnp.where(gb == 0, BLOCK, 0)
        hi = jnp.where(gb == n_blocks - 1, 2 * BLOCK, 3 * BLOCK)
        mask = band & (c_idx >= lo) & (c_idx < hi)
        r0 = pl.multiple_of(n * BLOCK, BLOCK)
        for kv in range(N_KV_HEADS):
            kcols = slice(kv * HEAD_DIM, (kv + 1) * HEAD_DIM)
            kb = kx_ref[pl.ds(r0, 3 * BLOCK), kcols]
            vb = vx_ref[pl.ds(r0, 3 * BLOCK), kcols]
            outs = []
            for hg in range(GROUP):
                h = kv * GROUP + hg
                qh = q_ref[0, pl.ds(r0, BLOCK), h * HEAD_DIM:(h + 1) * HEAD_DIM]
                s = lax.dot_general(qh, kb, (((1,), (1,)), ((), ())), preferred_element_type=F32)
                s = jnp.where(mask, s * (HEAD_DIM ** -0.5), NEG_INF)
                sk = sink_ref[h]
                m = jnp.maximum(jnp.max(s, axis=-1, keepdims=True), sk)
                e = jnp.exp(s - m)
                denom = jnp.sum(e, axis=-1, keepdims=True) + jnp.exp(sk - m)
                p = (e / denom).astype(BF16)
                outs.append(jnp.dot(p, vb, preferred_element_type=F32))
            oh_ref[pl.ds(r0, BLOCK), kv * GROUP * HEAD_DIM:(kv + 1) * GROUP * HEAD_DIM] = (
                jnp.concatenate(outs, axis=1).astype(BF16))
        return carry

    lax.fori_loop(0, blocks_per_tile, block, 0)
    att = jnp.dot(oh_ref[...], wo_ref[...], preferred_element_type=F32)
    o_ref[0] = x_ref[0] + _rms(att, g_ref[...])


def _attn(x, q, k, v, w_o, sink, g):
    b, s, _ = x.shape
    tq = ATT_TQ
    blocks_per_tile = tq // BLOCK
    n_blocks = s // BLOCK
    main = lambda bi, i: (bi, i, 0)
    prev = lambda bi, i: (bi, jnp.maximum(i * blocks_per_tile - 1, 0), 0)
    nxt = lambda bi, i: (bi, jnp.minimum((i + 1) * blocks_per_tile, n_blocks - 1), 0)
    kv_main = pl.BlockSpec((1, tq, KV_DIM), main)
    kv_prev = pl.BlockSpec((1, BLOCK, KV_DIM), prev)
    kv_next = pl.BlockSpec((1, BLOCK, KV_DIM), nxt)
    blocks = (2 * _nbytes((tq, D_MODEL), F32) + _nbytes((tq, Q_DIM), BF16)
              + 2 * _nbytes((tq + 2 * BLOCK, KV_DIM), BF16) + _nbytes((Q_DIM, D_MODEL), BF16)
              + _nbytes((1, D_MODEL), F32))
    scratch = 2 * _nbytes((tq + 2 * BLOCK, KV_DIM), BF16) + _nbytes((tq, Q_DIM), BF16)
    return pl.pallas_call(
        _attn_body,
        out_shape=jax.ShapeDtypeStruct((b, s, D_MODEL), F32),
        grid=(b, s // tq),
        in_specs=[
            pl.BlockSpec(memory_space=pltpu.SMEM),
            pl.BlockSpec((1, tq, D_MODEL), main),
            pl.BlockSpec((1, tq, Q_DIM), main),
            kv_prev, kv_main, kv_next,
            kv_prev, kv_main, kv_next,
            pl.BlockSpec((Q_DIM, D_MODEL), lambda bi, i: (0, 0)),
            pl.BlockSpec((1, D_MODEL), lambda bi, i: (0, 0)),
        ],
        out_specs=pl.BlockSpec((1, tq, D_MODEL), main),
        scratch_shapes=[pltpu.VMEM((tq + 2 * BLOCK, KV_DIM), BF16), pltpu.VMEM((tq + 2 * BLOCK, KV_DIM), BF16),
                        pltpu.VMEM((tq, Q_DIM), BF16)],
        compiler_params=pltpu.CompilerParams(
            dimension_semantics=("parallel", "arbitrary"),
            vmem_limit_bytes=_vmem_limit(blocks, scratch)),
        name="attn",
    )(sink, x, q, k, k, k, v, v, v, w_o, g)


def _trunk(x, p):
    b, s, _ = x.shape
    t = b * s
    row = lambda a: a.reshape(1, -1)
    x = x.reshape(t, D_MODEL)

    g = p["norm_g"][0]
    x = _ffn(x, row(g[0]), row(g[1]), p["wg"][0][0], p["wu"][0][0], p["wd"][0][0])
    h = _glu(x, row(g[2]), p["w_pw1"], row(p["b_pw1"]))
    x = _convmix(x.reshape(b, s, D_MODEL), h.reshape(b, s, D_MODEL), p["w_dw"], row(p["b_dw"]), row(p["ln_g"]),
                 row(p["ln_b"]), p["w_pw2"], row(p["b_pw2"]), row(g[3])).reshape(t, D_MODEL)
    x = _ffn(x, row(g[4]), row(g[5]), p["wg"][0][1], p["wu"][0][1], p["wd"][0][1])

    g = p["norm_g"][1]
    x = _ffn(x, row(g[0]), row(g[1]), p["wg"][1][0], p["wu"][1][0], p["wd"][1][0])
    q, k, v = _qkv(x, row(g[2]), p["w_qkv"], _rope_tables(s), s)
    x = _attn(x.reshape(b, s, D_MODEL), q.reshape(b, s, Q_DIM), k.reshape(b, s, KV_DIM), v.reshape(b, s, KV_DIM),
              p["w_o"], p["sink"], row(g[3])).reshape(t, D_MODEL)
    x = _ffn(x, row(g[4]), row(g[5]), p["wg"][1][1], p["wu"][1][1], p["wd"][1][1])
    return x.reshape(b, s, D_MODEL)


def kernel(x_prompt, x_sample, norm_g, ffn_w_gate, ffn_w_up, ffn_w_down, conv_w_pw1, conv_b_pw1, conv_w_dw,
           conv_b_dw, conv_ln_g, conv_ln_b, conv_w_pw2, conv_b_pw2, attn_w_qkv, attn_w_o, attn_sink):
    p = {
        "norm_g": norm_g,
        "wg": ffn_w_gate.astype(BF16), "wu": ffn_w_up.astype(BF16), "wd": ffn_w_down.astype(BF16),
        "w_pw1": conv_w_pw1[0].astype(BF16), "b_pw1": conv_b_pw1[0],
        "w_dw": conv_w_dw[0], "b_dw": conv_b_dw[0], "ln_g": conv_ln_g[0], "ln_b": conv_ln_b[0],
        "w_pw2": conv_w_pw2[0].astype(BF16), "b_pw2": conv_b_pw2[0],
        "w_qkv": attn_w_qkv[0].astype(BF16), "w_o": attn_w_o[0].astype(BF16), "sink": attn_sink[0],
    }
    return _trunk(x_prompt, p), _trunk(x_sample, p)
```

```python
import functools
import math

import jax
import jax.numpy as jnp
from jax import lax
from jax.experimental import pallas as pl
from jax.experimental.pallas import tpu as pltpu

F32 = jnp.float32
BF16 = jnp.bfloat16

D_MODEL = 1024
D_FF = 4096
CONV_KERNEL = 31
CONV_HALF = CONV_KERNEL // 2
HEAD_DIM = 64
N_HEADS = 16
N_KV_HEADS = 4
GROUP = N_HEADS // N_KV_HEADS
Q_DIM = N_HEADS * HEAD_DIM
KV_DIM = N_KV_HEADS * HEAD_DIM
QKV_DIM = Q_DIM + 2 * KV_DIM
WINDOW = 128
BLOCK = 128
ROPE_THETA = 10000.0
EPS = 1e-6
NEG_INF = -1e30
N_SUBLAYER_NORMS = 6

LANES = 128
SUBLANES = 8
HALO_ROWS = 16
K_PAD_DIM = N_KV_HEADS * LANES

FFN_TM = 1024
FFN_TF = 512
GLU_TM = 512
CONV_TM = 512
CONV_ROWS = 64
QKV_TM = 512
ATT_TQ = 512

VMEM_TEMP_BYTES = 16 * 1024 * 1024


def _vmem_limit(pipelined_bytes, scratch_bytes):
    return int(2 * pipelined_bytes + scratch_bytes + VMEM_TEMP_BYTES)


def _nbytes(shape, dtype):
    return math.prod(shape) * jnp.dtype(dtype).itemsize


def _rms(x, g):
    ms = jnp.mean(x * x, axis=-1, keepdims=True)
    return x * lax.rsqrt(ms + EPS) * g


def _sigmoid(x):
    return 1.0 / (1.0 + jnp.exp(-x))


def _row_spec(k, width=D_MODEL):
    return pl.BlockSpec((None, 1, width), lambda *_: (k, 0, 0))


def _mat_spec(k, rows, cols):
    return pl.BlockSpec((None, rows, cols), lambda *_: (k, 0, 0))


def _ffn_body(x_ref, gpre_ref, gpost_ref, wg_ref, wu_ref, wd_ref, o_ref, xn_ref, acc_ref):
    j = pl.program_id(1)

    @pl.when(j == 0)
    def _():
        xn_ref[...] = _rms(x_ref[...], gpre_ref[...]).astype(BF16)
        acc_ref[...] = jnp.zeros_like(acc_ref)

    xn = xn_ref[...]
    gate = jnp.dot(xn, wg_ref[...], preferred_element_type=F32)
    up = jnp.dot(xn, wu_ref[...], preferred_element_type=F32)
    h = (gate * _sigmoid(gate) * up).astype(BF16)
    acc_ref[...] += jnp.dot(h, wd_ref[...], preferred_element_type=F32)

    @pl.when(j == pl.num_programs(1) - 1)
    def _():
        o_ref[...] = x_ref[...] + 0.5 * _rms(acc_ref[...], gpost_ref[...])


def _ffn(x, gains, k_pre, wg, wu, wd, layer, which):
    t = x.shape[0]
    tm, tf = FFN_TM, FFN_TF
    blocks = (_nbytes((tm, D_MODEL), F32) * 2 + 2 * _nbytes((D_MODEL, tf), BF16)
              + _nbytes((tf, D_MODEL), BF16) + 2 * _nbytes((1, D_MODEL), F32))
    scratch = _nbytes((tm, D_MODEL), BF16) + _nbytes((tm, D_MODEL), F32)
    return pl.pallas_call(
        _ffn_body,
        out_shape=jax.ShapeDtypeStruct((t, D_MODEL), F32),
        grid=(t // tm, D_FF // tf),
        in_specs=[
            pl.BlockSpec((tm, D_MODEL), lambda i, j: (i, 0)),
            _row_spec(k_pre),
            _row_spec(k_pre + 1),
            pl.BlockSpec((None, None, D_MODEL, tf), lambda i, j: (layer, which, 0, j)),
            pl.BlockSpec((None, None, D_MODEL, tf), lambda i, j: (layer, which, 0, j)),
            pl.BlockSpec((None, None, tf, D_MODEL), lambda i, j: (layer, which, j, 0)),
        ],
        out_specs=pl.BlockSpec((tm, D_MODEL), lambda i, j: (i, 0)),
        scratch_shapes=[pltpu.VMEM((tm, D_MODEL), BF16), pltpu.VMEM((tm, D_MODEL), F32)],
        compiler_params=pltpu.CompilerParams(
            dimension_semantics=("parallel", "arbitrary"),
            vmem_limit_bytes=_vmem_limit(blocks, scratch)),
        name="ffn",
    )(x, gains, gains, wg, wu, wd)


def _glu_body(x_ref, g_ref, w_ref, b_ref, o_ref):
    hn = _rms(x_ref[...], g_ref[...]).astype(BF16)
    h = jnp.dot(hn, w_ref[...], preferred_element_type=F32) + b_ref[...]
    o_ref[...] = h[:, :D_MODEL] * _sigmoid(h[:, D_MODEL:])


def _glu(x, gains, k_gain, w_pw1, b_pw1, j):
    t = x.shape[0]
    tm = GLU_TM
    blocks = (2 * _nbytes((tm, D_MODEL), F32) + _nbytes((D_MODEL, 2 * D_MODEL), BF16)
              + 3 * _nbytes((1, D_MODEL), F32))
    return pl.pallas_call(
        _glu_body,
        out_shape=jax.ShapeDtypeStruct((t, D_MODEL), F32),
        grid=(t // tm,),
        in_specs=[
            pl.BlockSpec((tm, D_MODEL), lambda i: (i, 0)),
            _row_spec(k_gain),
            _mat_spec(j, D_MODEL, 2 * D_MODEL),
            _row_spec(j, 2 * D_MODEL),
        ],
        out_specs=pl.BlockSpec((tm, D_MODEL), lambda i: (i, 0)),
        compiler_params=pltpu.CompilerParams(
            dimension_semantics=("parallel",),
            vmem_limit_bytes=_vmem_limit(blocks, 0)),
        name="glu",
    )(x, gains, w_pw1, b_pw1)


def _convmix_body(x_ref, h_ref, hprev_ref, hnext_ref, wdw_ref, bdw_ref, lng_ref, lnb_ref,
                  w2_ref, b2_ref, g_ref, o_ref, hp_ref, c_ref):
    i = pl.program_id(1)
    tm = h_ref.shape[1]
    hp_ref[0:HALO_ROWS] = jnp.where(i > 0, hprev_ref[0], 0.0)
    hp_ref[HALO_ROWS:HALO_ROWS + tm] = h_ref[0]
    hp_ref[HALO_ROWS + tm:] = jnp.where(i < pl.num_programs(1) - 1, hnext_ref[0], 0.0)

    base = HALO_ROWS - CONV_HALF
    win = CONV_ROWS + 2 * HALO_ROWS
    for c in range(D_MODEL // LANES):
        cols = slice(c * LANES, (c + 1) * LANES)

        def rows(r, carry, cols=cols):
            r0 = pl.multiple_of(r * CONV_ROWS, CONV_ROWS)
            v = hp_ref[pl.ds(r0, win), cols]
            acc = None
            for shift in range(SUBLANES):
                vs = v if shift == 0 else pltpu.roll(v, win - shift, 0)
                for tile in range(2 * HALO_ROWS // SUBLANES):
                    k = tile * SUBLANES + shift - base
                    if 0 <= k < CONV_KERNEL:
                        term = vs[tile * SUBLANES:tile * SUBLANES + CONV_ROWS] * wdw_ref[k:k + 1, cols]
                        acc = term if acc is None else acc + term
            c_ref[pl.ds(r0, CONV_ROWS), cols] = acc
            return carry

        lax.fori_loop(0, tm // CONV_ROWS, rows, 0)

    y = c_ref[...] + bdw_ref[...]
    mu = jnp.mean(y, axis=-1, keepdims=True)
    yc = y - mu
    var = jnp.mean(yc * yc, axis=-1, keepdims=True)
    y = yc * lax.rsqrt(var + EPS) * lng_ref[...] + lnb_ref[...]
    y = (y * _sigmoid(y)).astype(BF16)
    h = jnp.dot(y, w2_ref[...], preferred_element_type=F32) + b2_ref[...]
    o_ref[0] = x_ref[0] + _rms(h, g_ref[...])


def _convmix(x, h, gains, k_gain, w_dw, b_dw, ln_g, ln_b, w_pw2, b_pw2, j):
    b, s, _ = x.shape
    tm = CONV_TM
    halo_per_tile = tm // HALO_ROWS
    n_halo = s // HALO_ROWS
    blocks = (3 * _nbytes((tm, D_MODEL), F32) + 2 * _nbytes((HALO_ROWS, D_MODEL), F32)
              + _nbytes((CONV_KERNEL, D_MODEL), F32) + _nbytes((D_MODEL, D_MODEL), BF16)
              + 6 * _nbytes((1, D_MODEL), F32))
    scratch = _nbytes((tm + 2 * HALO_ROWS, D_MODEL), F32) + _nbytes((tm, D_MODEL), F32)
    return pl.pallas_call(
        _convmix_body,
        out_shape=jax.ShapeDtypeStruct((b, s, D_MODEL), F32),
        grid=(b, s // tm),
        in_specs=[
            pl.BlockSpec((1, tm, D_MODEL), lambda bi, i: (bi, i, 0)),
            pl.BlockSpec((1, tm, D_MODEL), lambda bi, i: (bi, i, 0)),
            pl.BlockSpec((1, HALO_ROWS, D_MODEL),
                         lambda bi, i: (bi, jnp.maximum(i * halo_per_tile - 1, 0), 0)),
            pl.BlockSpec((1, HALO_ROWS, D_MODEL),
                         lambda bi, i: (bi, jnp.minimum((i + 1) * halo_per_tile, n_halo - 1), 0)),
            _mat_spec(j, CONV_KERNEL, D_MODEL),
            _row_spec(j), _row_spec(j), _row_spec(j),
            _mat_spec(j, D_MODEL, D_MODEL),
            _row_spec(j),
            _row_spec(k_gain),
        ],
        out_specs=pl.BlockSpec((1, tm, D_MODEL), lambda bi, i: (bi, i, 0)),
        scratch_shapes=[pltpu.VMEM((tm + 2 * HALO_ROWS, D_MODEL), F32), pltpu.VMEM((tm, D_MODEL), F32)],
        compiler_params=pltpu.CompilerParams(
            dimension_semantics=("parallel", "arbitrary"),
            vmem_limit_bytes=_vmem_limit(blocks, scratch)),
        name="convmix",
    )(x, h, h, h, w_dw, b_dw, ln_g, ln_b, w_pw2, b_pw2, gains)


def _rope_tables(s):
    half = HEAD_DIM // 2
    inv = ROPE_THETA ** (-jnp.arange(half, dtype=F32) / half)
    ang = jnp.arange(s).astype(F32)[:, None] * inv[None, :]
    cos, sin, zero = jnp.cos(ang), jnp.sin(ang), jnp.zeros_like(ang)
    reps = LANES // HEAD_DIM
    cos_t = jnp.tile(jnp.concatenate([cos, cos], axis=-1), (1, reps))
    sin_lo = jnp.tile(jnp.concatenate([-sin, zero], axis=-1), (1, reps))
    sin_hi = jnp.tile(jnp.concatenate([zero, sin], axis=-1), (1, reps))
    return cos_t, sin_lo, sin_hi


def _qkv_body(x_ref, g_ref, w_ref, cos_ref, slo_ref, shi_ref, qt_ref, kp_ref, vt_ref):
    tm = x_ref.shape[0]
    hn = _rms(x_ref[...], g_ref[...]).astype(BF16)
    qkv = jnp.dot(hn, w_ref[...], preferred_element_type=F32)
    cos, slo, shi = cos_ref[...], slo_ref[...], shi_ref[...]
    half = HEAD_DIM // 2
    heads_per_group = LANES // HEAD_DIM

    def rope(xc):
        return xc * cos + pltpu.roll(xc, LANES - half, 1) * slo + pltpu.roll(xc, half, 1) * shi

    for c in range(Q_DIM // LANES):
        qc = rope(qkv[:, c * LANES:(c + 1) * LANES])
        for n in range(tm // BLOCK):
            qct = qc[n * BLOCK:(n + 1) * BLOCK].T.astype(BF16)
            for hl in range(heads_per_group):
                head = c * heads_per_group + hl
                kv, hg = head // GROUP, head % GROUP
                qt_ref[n, kv, :, hg * BLOCK:(hg + 1) * BLOCK] = qct[hl * HEAD_DIM:(hl + 1) * HEAD_DIM]
    for c in range(KV_DIM // LANES):
        kc = rope(qkv[:, Q_DIM + c * LANES:Q_DIM + (c + 1) * LANES])
        kp_ref[:, (2 * c) * LANES:(2 * c + 1) * LANES] = kc.astype(BF16)
        kp_ref[:, (2 * c + 1) * LANES:(2 * c + 2) * LANES] = pltpu.roll(kc, HEAD_DIM, 1).astype(BF16)
        vc = qkv[:, Q_DIM + KV_DIM + c * LANES:Q_DIM + KV_DIM + (c + 1) * LANES]
        for n in range(tm // BLOCK):
            vt_ref[c * LANES:(c + 1) * LANES, n * BLOCK:(n + 1) * BLOCK] = (
                vc[n * BLOCK:(n + 1) * BLOCK].T.astype(BF16))


def _qkv(x, gains, k_gain, w_qkv, j, tables, b, s):
    t = x.shape[0]
    tm = QKV_TM
    tiles_per_seq = s // tm
    blocks = (_nbytes((tm, D_MODEL), F32) + _nbytes((D_MODEL, QKV_DIM), BF16) + 3 * _nbytes((tm, LANES), F32)
              + _nbytes((tm, Q_DIM + K_PAD_DIM + KV_DIM), BF16) + _nbytes((1, D_MODEL), F32))
    tab = pl.BlockSpec((tm, LANES), lambda i: (i % tiles_per_seq, 0))
    return pl.pallas_call(
        _qkv_body,
        out_shape=(jax.ShapeDtypeStruct((t // BLOCK, N_KV_HEADS, HEAD_DIM, GROUP * BLOCK), BF16),
                   jax.ShapeDtypeStruct((t, K_PAD_DIM), BF16),
                   jax.ShapeDtypeStruct((b, KV_DIM, s), BF16)),
        grid=(t // tm,),
        in_specs=[
            pl.BlockSpec((tm, D_MODEL), lambda i: (i, 0)),
            _row_spec(k_gain),
            _mat_spec(j, D_MODEL, QKV_DIM),
            tab, tab, tab,
        ],
        out_specs=(pl.BlockSpec((tm // BLOCK, N_KV_HEADS, HEAD_DIM, GROUP * BLOCK), lambda i: (i, 0, 0, 0)),
                   pl.BlockSpec((tm, K_PAD_DIM), lambda i: (i, 0)),
                   pl.BlockSpec((None, KV_DIM, tm), lambda i: (i // tiles_per_seq, 0, i % tiles_per_seq))),
        compiler_params=pltpu.CompilerParams(
            dimension_semantics=("parallel",),
            vmem_limit_bytes=_vmem_limit(blocks, 0)),
        name="qkv",
    )(x, gains, w_qkv, *tables)


def _attn_body(sink_ref, x_ref, qt_ref, kprev_ref, k_ref, knext_ref, vprev_ref, v_ref, vnext_ref, wo_ref, g_ref,
               o_ref, kx_ref, vx_ref, oh_ref, *, layer):
    i = pl.program_id(1)
    tq = x_ref.shape[1]
    blocks_per_tile = tq // BLOCK
    n_blocks = pl.num_programs(1) * blocks_per_tile
    kx_ref[0:BLOCK] = kprev_ref[0]
    kx_ref[BLOCK:BLOCK + tq] = k_ref[0]
    kx_ref[BLOCK + tq:] = knext_ref[0]
    vx_ref[:, 0:BLOCK] = vprev_ref[0]
    vx_ref[:, BLOCK:BLOCK + tq] = v_ref[0]
    vx_ref[:, BLOCK + tq:] = vnext_ref[0]

    log2e = math.log2(math.e)
    scale = HEAD_DIM ** -0.5 * log2e
    key = lax.broadcasted_iota(jnp.int32, (BLOCK, BLOCK), 0)
    qry = lax.broadcasted_iota(jnp.int32, (BLOCK, BLOCK), 1)

    for n in range(blocks_per_tile):
        gb = i * blocks_per_tile + n
        off_prev = jnp.where(gb == 0, BLOCK, BLOCK - WINDOW)
        off_next = jnp.where(gb == n_blocks - 1, BLOCK, BLOCK - WINDOW)
        mask_prev = jnp.concatenate([key >= qry + off_prev] * GROUP, axis=1)
        mask_next = jnp.concatenate([key <= qry - off_next] * GROUP, axis=1)
        rows = slice(n * BLOCK, (n + 3) * BLOCK)
        for kv in range(N_KV_HEADS):
            kb = kx_ref[rows, kv * LANES:kv * LANES + HEAD_DIM]
            t = jnp.dot(kb, qt_ref[n, kv], preferred_element_type=F32) * scale
            t_prev = jnp.where(mask_prev, t[0:BLOCK], NEG_INF)
            t_own = t[BLOCK:2 * BLOCK]
            t_next = jnp.where(mask_next, t[2 * BLOCK:], NEG_INF)
            sink = jnp.concatenate(
                [jnp.full((1, BLOCK), sink_ref[layer, kv * GROUP + hg] * log2e, F32) for hg in range(GROUP)], axis=1)
            m = jnp.maximum(jnp.maximum(t_prev, t_own), t_next)
            m = jnp.maximum(jnp.max(m, axis=0, keepdims=True), sink)
            e_prev, e_own, e_next = jnp.exp2(t_prev - m), jnp.exp2(t_own - m), jnp.exp2(t_next - m)
            denom = jnp.sum(e_prev + e_own + e_next, axis=0, keepdims=True) + jnp.exp2(sink - m)
            p = jnp.concatenate([e_prev, e_own, e_next], axis=0).astype(BF16)
            vb = vx_ref[kv * HEAD_DIM:(kv + 1) * HEAD_DIM, rows]
            ot = jnp.dot(vb, p, preferred_element_type=F32) * (1.0 / denom)
            for pair in range(GROUP // 2):
                two = jnp.concatenate([ot[:, (2 * pair) * BLOCK:(2 * pair + 1) * BLOCK],
                                       ot[:, (2 * pair + 1) * BLOCK:(2 * pair + 2) * BLOCK]], axis=0)
                col = (kv * GROUP + 2 * pair) * HEAD_DIM
                oh_ref[n * BLOCK:(n + 1) * BLOCK, col:col + 2 * HEAD_DIM] = two.T.astype(BF16)

    att = jnp.dot(oh_ref[...], wo_ref[...], preferred_element_type=F32)
    o_ref[0] = x_ref[0] + _rms(att, g_ref[...])


def _attn(x, qt, kp, vt, gains, k_gain, w_o, sink, j):
    b, s, _ = x.shape
    tq = ATT_TQ
    blocks_per_tile = tq // BLOCK
    tiles_per_seq = s // tq
    n_blocks = s // BLOCK
    main = lambda bi, i: (bi, i, 0)
    prev_blk = lambda i: jnp.maximum(i * blocks_per_tile - 1, 0)
    next_blk = lambda i: jnp.minimum((i + 1) * blocks_per_tile, n_blocks - 1)
    kp = kp.reshape(b, s, K_PAD_DIM)
    blocks = (2 * _nbytes((tq, D_MODEL), F32) + _nbytes((tq, Q_DIM), BF16)
              + _nbytes((tq + 2 * BLOCK, K_PAD_DIM + KV_DIM), BF16) + _nbytes((Q_DIM, D_MODEL), BF16)
              + _nbytes((1, D_MODEL), F32))
    scratch = _nbytes((tq + 2 * BLOCK, K_PAD_DIM + KV_DIM), BF16) + _nbytes((tq, Q_DIM), BF16)
    return pl.pallas_call(
        functools.partial(_attn_body, layer=j),
        out_shape=jax.ShapeDtypeStruct((b, s, D_MODEL), F32),
        grid=(b, tiles_per_seq),
        in_specs=[
            pl.BlockSpec(memory_space=pltpu.SMEM),
            pl.BlockSpec((1, tq, D_MODEL), main),
            pl.BlockSpec((blocks_per_tile, N_KV_HEADS, HEAD_DIM, GROUP * BLOCK),
                         lambda bi, i: (bi * tiles_per_seq + i, 0, 0, 0)),
            pl.BlockSpec((1, BLOCK, K_PAD_DIM), lambda bi, i: (bi, prev_blk(i), 0)),
            pl.BlockSpec((1, tq, K_PAD_DIM), main),
            pl.BlockSpec((1, BLOCK, K_PAD_DIM), lambda bi, i: (bi, next_blk(i), 0)),
            pl.BlockSpec((1, KV_DIM, BLOCK), lambda bi, i: (bi, 0, prev_blk(i))),
            pl.BlockSpec((1, KV_DIM, tq), lambda bi, i: (bi, 0, i)),
            pl.BlockSpec((1, KV_DIM, BLOCK), lambda bi, i: (bi, 0, next_blk(i))),
            _mat_spec(j, Q_DIM, D_MODEL),
            _row_spec(k_gain),
        ],
        out_specs=pl.BlockSpec((1, tq, D_MODEL), main),
        scratch_shapes=[pltpu.VMEM((tq + 2 * BLOCK, K_PAD_DIM), BF16), pltpu.VMEM((KV_DIM, tq + 2 * BLOCK), BF16),
                        pltpu.VMEM((tq, Q_DIM), BF16)],
        compiler_params=pltpu.CompilerParams(
            dimension_semantics=("parallel", "arbitrary"),
            vmem_limit_bytes=_vmem_limit(blocks, scratch)),
        name="attn",
    )(sink, x, qt, kp, kp, kp, vt, vt, vt, w_o, gains)


def _trunk(x, p):
    b, s, _ = x.shape
    t = b * s
    gains = p["gains"]
    x = x.reshape(t, D_MODEL)

    def ffn(x, layer, which):
        k_pre = layer * N_SUBLAYER_NORMS + 4 * which
        return _ffn(x, gains, k_pre, p["wg"], p["wu"], p["wd"], layer, which)

    x = ffn(x, 0, 0)
    h = _glu(x, gains, 2, p["w_pw1"], p["b_pw1"], 0)
    x = _convmix(x.reshape(b, s, D_MODEL), h.reshape(b, s, D_MODEL), gains, 3, p["w_dw"], p["b_dw"], p["ln_g"],
                 p["ln_b"], p["w_pw2"], p["b_pw2"], 0).reshape(t, D_MODEL)
    x = ffn(x, 0, 1)

    x = ffn(x, 1, 0)
    qt, kp, vt = _qkv(x, gains, N_SUBLAYER_NORMS + 2, p["w_qkv"], 0, _rope_tables(s), b, s)
    x = _attn(x.reshape(b, s, D_MODEL), qt, kp, vt, gains, N_SUBLAYER_NORMS + 3, p["w_o"], p["sink"],
              0).reshape(t, D_MODEL)
    x = ffn(x, 1, 1)
    return x.reshape(b, s, D_MODEL)


def kernel(x_prompt, x_sample, norm_g, ffn_w_gate, ffn_w_up, ffn_w_down, conv_w_pw1, conv_b_pw1, conv_w_dw,
           conv_b_dw, conv_ln_g, conv_ln_b, conv_w_pw2, conv_b_pw2, attn_w_qkv, attn_w_o, attn_sink):
    rows = lambda a: a.reshape(-1, 1, a.shape[-1])
    p = {
        "gains": rows(norm_g),
        "wg": ffn_w_gate.astype(BF16), "wu": ffn_w_up.astype(BF16), "wd": ffn_w_down.astype(BF16),
        "w_pw1": conv_w_pw1.astype(BF16), "b_pw1": rows(conv_b_pw1),
        "w_dw": conv_w_dw, "b_dw": rows(conv_b_dw), "ln_g": rows(conv_ln_g), "ln_b": rows(conv_ln_b),
        "w_pw2": conv_w_pw2.astype(BF16), "b_pw2": rows(conv_b_pw2),
        "w_qkv": attn_w_qkv.astype(BF16), "w_o": attn_w_o.astype(BF16), "sink": attn_sink,
    }
    return _trunk(x_prompt, p), _trunk(x_sample, p)
```

```python
import functools
import math

import jax
import jax.numpy as jnp
from jax import lax
from jax.experimental import pallas as pl
from jax.experimental.pallas import tpu as pltpu

F32 = jnp.float32
BF16 = jnp.bfloat16

D_MODEL = 1024
D_FF = 4096
CONV_KERNEL = 31
CONV_HALF = CONV_KERNEL // 2
HEAD_DIM = 64
N_HEADS = 16
N_KV_HEADS = 4
GROUP = N_HEADS // N_KV_HEADS
Q_DIM = N_HEADS * HEAD_DIM
KV_DIM = N_KV_HEADS * HEAD_DIM
QKV_DIM = Q_DIM + 2 * KV_DIM
WINDOW = 128
BLOCK = 128
ROPE_THETA = 10000.0
EPS = 1e-6
NEG_INF = -1e30
N_SUBLAYER_NORMS = 6

LANES = 128
SUBLANES = 8
HALO_ROWS = 16
K_PAD_DIM = N_KV_HEADS * LANES

FFN_TM = 512
FFN_TF = 512
GLU_TM = 512
CONV_TM = 512
CONV_ROWS = 128
QKV_TM = 512
ATT_TQ = 512

VMEM_TEMP_BYTES = 16 * 1024 * 1024


def _vmem_limit(pipelined_bytes, scratch_bytes):
    return int(2 * pipelined_bytes + scratch_bytes + VMEM_TEMP_BYTES)


def _nbytes(shape, dtype):
    return math.prod(shape) * jnp.dtype(dtype).itemsize


def _rms(x, g):
    ms = jnp.mean(x * x, axis=-1, keepdims=True)
    return x * lax.rsqrt(ms + EPS) * g


def _sigmoid(x):
    return 1.0 / (1.0 + jnp.exp(-x))


def _row_spec(k, width=D_MODEL):
    return pl.BlockSpec((None, 1, width), lambda *_: (k, 0, 0))


def _mat_spec(k, rows, cols):
    return pl.BlockSpec((None, rows, cols), lambda *_: (k, 0, 0))


def _ffn_body(x_ref, gpre_ref, gpost_ref, wg_ref, wu_ref, wd_ref, o_ref):
    x = x_ref[...]
    xn = _rms(x, gpre_ref[...]).astype(BF16)
    acc = None
    for c in range(D_FF // FFN_TF):
        cols = slice(c * FFN_TF, (c + 1) * FFN_TF)
        gate = jnp.dot(xn, wg_ref[:, cols], preferred_element_type=F32)
        up = jnp.dot(xn, wu_ref[:, cols], preferred_element_type=F32)
        h = (gate * _sigmoid(gate) * up).astype(BF16)
        part = jnp.dot(h, wd_ref[cols, :], preferred_element_type=F32)
        acc = part if acc is None else acc + part
    o_ref[...] = x + 0.5 * _rms(acc, gpost_ref[...])


def _ffn(x, gains, k_pre, wg, wu, wd, layer, which):
    t = x.shape[0]
    tm = FFN_TM
    resident = pl.Buffered(1)
    weights = 3 * _nbytes((D_MODEL, D_FF), BF16)
    blocks = 2 * _nbytes((tm, D_MODEL), F32) + 2 * _nbytes((1, D_MODEL), F32)
    return pl.pallas_call(
        _ffn_body,
        out_shape=jax.ShapeDtypeStruct((t, D_MODEL), F32),
        grid=(t // tm,),
        in_specs=[
            pl.BlockSpec((tm, D_MODEL), lambda i: (i, 0)),
            _row_spec(k_pre),
            _row_spec(k_pre + 1),
            pl.BlockSpec((None, None, D_MODEL, D_FF), lambda i: (layer, which, 0, 0), pipeline_mode=resident),
            pl.BlockSpec((None, None, D_MODEL, D_FF), lambda i: (layer, which, 0, 0), pipeline_mode=resident),
            pl.BlockSpec((None, None, D_FF, D_MODEL), lambda i: (layer, which, 0, 0), pipeline_mode=resident),
        ],
        out_specs=pl.BlockSpec((tm, D_MODEL), lambda i: (i, 0)),
        compiler_params=pltpu.CompilerParams(
            dimension_semantics=("parallel",),
            vmem_limit_bytes=_vmem_limit(blocks, weights)),
        name="ffn",
    )(x, gains, gains, wg, wu, wd)


def _glu_body(x_ref, g_ref, w_ref, b_ref, o_ref):
    hn = _rms(x_ref[...], g_ref[...]).astype(BF16)
    h = jnp.dot(hn, w_ref[...], preferred_element_type=F32) + b_ref[...]
    o_ref[...] = h[:, :D_MODEL] * _sigmoid(h[:, D_MODEL:])


def _glu(x, gains, k_gain, w_pw1, b_pw1, j):
    t = x.shape[0]
    tm = GLU_TM
    blocks = (2 * _nbytes((tm, D_MODEL), F32) + _nbytes((D_MODEL, 2 * D_MODEL), BF16)
              + 3 * _nbytes((1, D_MODEL), F32))
    return pl.pallas_call(
        _glu_body,
        out_shape=jax.ShapeDtypeStruct((t, D_MODEL), F32),
        grid=(t // tm,),
        in_specs=[
            pl.BlockSpec((tm, D_MODEL), lambda i: (i, 0)),
            _row_spec(k_gain),
            _mat_spec(j, D_MODEL, 2 * D_MODEL),
            _row_spec(j, 2 * D_MODEL),
        ],
        out_specs=pl.BlockSpec((tm, D_MODEL), lambda i: (i, 0)),
        compiler_params=pltpu.CompilerParams(
            dimension_semantics=("parallel",),
            vmem_limit_bytes=_vmem_limit(blocks, 0)),
        name="glu",
    )(x, gains, w_pw1, b_pw1)


def _convmix_body(x_ref, h_ref, hprev_ref, hnext_ref, wdw_ref, bdw_ref, lng_ref, lnb_ref,
                  w2_ref, b2_ref, g_ref, o_ref, hp_ref, c_ref):
    i = pl.program_id(1)
    tm = h_ref.shape[1]
    pieces = ((0, jnp.where(i > 0, hprev_ref[0], 0.0)),
              (HALO_ROWS, h_ref[0]),
              (HALO_ROWS + tm, jnp.where(i < pl.num_programs(1) - 1, hnext_ref[0], 0.0)))
    for row0, piece in pieces:
        for c in range(D_MODEL // LANES):
            hp_ref[c // 2, pl.ds(2 * row0 + c % 2, piece.shape[0], stride=2), :] = piece[:, c * LANES:(c + 1) * LANES]

    base = HALO_ROWS - CONV_HALF
    n_tiles = -(-(base + CONV_KERNEL) // SUBLANES)
    win = CONV_ROWS + (n_tiles - 1) * SUBLANES
    for c in range(D_MODEL // LANES):
        cols = slice(c * LANES, (c + 1) * LANES)

        def rows(r, carry, c=c, cols=cols):
            r0 = pl.multiple_of(r * CONV_ROWS, CONV_ROWS)
            acc = None
            for shift in range(SUBLANES):
                vs = hp_ref[c // 2, pl.ds(2 * (r0 + shift) + c % 2, win, stride=2), :]
                for tile in range(n_tiles):
                    k = tile * SUBLANES + shift - base
                    if 0 <= k < CONV_KERNEL:
                        term = vs[tile * SUBLANES:tile * SUBLANES + CONV_ROWS] * wdw_ref[k:k + 1, cols]
                        acc = term if acc is None else acc + term
            c_ref[pl.ds(r0, CONV_ROWS), cols] = acc
            return carry

        lax.fori_loop(0, tm // CONV_ROWS, rows, 0)

    y = c_ref[...] + bdw_ref[...]
    mu = jnp.mean(y, axis=-1, keepdims=True)
    yc = y - mu
    var = jnp.mean(yc * yc, axis=-1, keepdims=True)
    y = yc * lax.rsqrt(var + EPS) * lng_ref[...] + lnb_ref[...]
    y = (y * _sigmoid(y)).astype(BF16)
    h = jnp.dot(y, w2_ref[...], preferred_element_type=F32) + b2_ref[...]
    o_ref[0] = x_ref[0] + _rms(h, g_ref[...])


def _convmix(x, h, gains, k_gain, w_dw, b_dw, ln_g, ln_b, w_pw2, b_pw2, j):
    b, s, _ = x.shape
    tm = CONV_TM
    halo_per_tile = tm // HALO_ROWS
    n_halo = s // HALO_ROWS
    blocks = (3 * _nbytes((tm, D_MODEL), F32) + 2 * _nbytes((HALO_ROWS, D_MODEL), F32)
              + _nbytes((CONV_KERNEL, D_MODEL), F32) + _nbytes((D_MODEL, D_MODEL), BF16)
              + 6 * _nbytes((1, D_MODEL), F32))
    scratch = _nbytes((tm + 2 * HALO_ROWS, D_MODEL), F32) + _nbytes((tm, D_MODEL), F32)
    return pl.pallas_call(
        _convmix_body,
        out_shape=jax.ShapeDtypeStruct((b, s, D_MODEL), F32),
        grid=(b, s // tm),
        in_specs=[
            pl.BlockSpec((1, tm, D_MODEL), lambda bi, i: (bi, i, 0)),
            pl.BlockSpec((1, tm, D_MODEL), lambda bi, i: (bi, i, 0)),
            pl.BlockSpec((1, HALO_ROWS, D_MODEL),
                         lambda bi, i: (bi, jnp.maximum(i * halo_per_tile - 1, 0), 0)),
            pl.BlockSpec((1, HALO_ROWS, D_MODEL),
                         lambda bi, i: (bi, jnp.minimum((i + 1) * halo_per_tile, n_halo - 1), 0)),
            _mat_spec(j, CONV_KERNEL, D_MODEL),
            _row_spec(j), _row_spec(j), _row_spec(j),
            _mat_spec(j, D_MODEL, D_MODEL),
            _row_spec(j),
            _row_spec(k_gain),
        ],
        out_specs=pl.BlockSpec((1, tm, D_MODEL), lambda bi, i: (bi, i, 0)),
        scratch_shapes=[pltpu.VMEM((D_MODEL // (2 * LANES), 2 * (tm + 2 * HALO_ROWS), LANES), F32),
                        pltpu.VMEM((tm, D_MODEL), F32)],
        compiler_params=pltpu.CompilerParams(
            dimension_semantics=("parallel", "arbitrary"),
            vmem_limit_bytes=_vmem_limit(blocks, scratch)),
        name="convmix",
    )(x, h, h, h, w_dw, b_dw, ln_g, ln_b, w_pw2, b_pw2, gains)


def _rope_tables(s):
    half = HEAD_DIM // 2
    inv = ROPE_THETA ** (-jnp.arange(half, dtype=F32) / half)
    ang = jnp.arange(s).astype(F32)[:, None] * inv[None, :]
    cos, sin, zero = jnp.cos(ang), jnp.sin(ang), jnp.zeros_like(ang)
    reps = LANES // HEAD_DIM
    cos_row = jnp.tile(jnp.concatenate([cos, cos], axis=-1), (1, reps))
    sin_lo = jnp.tile(jnp.concatenate([-sin, zero], axis=-1), (1, reps))
    sin_hi = jnp.tile(jnp.concatenate([zero, sin], axis=-1), (1, reps))
    return cos_row, sin_lo, sin_hi, cos.T, sin.T


def _qkv_body(x_ref, g_ref, w_ref, cos_ref, slo_ref, shi_ref, cost_ref, sint_ref, qt_ref, kp_ref, vt_ref):
    tm = x_ref.shape[0]
    hn = _rms(x_ref[...], g_ref[...]).astype(BF16)
    qkv = jnp.dot(hn, w_ref[...], preferred_element_type=F32)
    half = HEAD_DIM // 2
    heads_per_group = LANES // HEAD_DIM

    for c in range(Q_DIM // LANES):
        qc = qkv[:, c * LANES:(c + 1) * LANES]
        for n in range(tm // BLOCK):
            qct = qc[n * BLOCK:(n + 1) * BLOCK].T
            cos = cost_ref[:, n * BLOCK:(n + 1) * BLOCK]
            sin = sint_ref[:, n * BLOCK:(n + 1) * BLOCK]
            for hl in range(heads_per_group):
                head = c * heads_per_group + hl
                kv, hg = head // GROUP, head % GROUP
                x1 = qct[hl * HEAD_DIM:hl * HEAD_DIM + half]
                x2 = qct[hl * HEAD_DIM + half:(hl + 1) * HEAD_DIM]
                roped = jnp.concatenate([x1 * cos - x2 * sin, x2 * cos + x1 * sin], axis=0)
                qt_ref[n, kv, :, hg * BLOCK:(hg + 1) * BLOCK] = roped.astype(BF16)

    cos, slo, shi = cos_ref[...], slo_ref[...], shi_ref[...]
    for c in range(KV_DIM // LANES):
        kc = qkv[:, Q_DIM + c * LANES:Q_DIM + (c + 1) * LANES]
        kc = kc * cos + pltpu.roll(kc, LANES - half, 1) * slo + pltpu.roll(kc, half, 1) * shi
        kp_ref[:, (2 * c) * LANES:(2 * c + 1) * LANES] = kc.astype(BF16)
        kp_ref[:, (2 * c + 1) * LANES:(2 * c + 2) * LANES] = pltpu.roll(kc, HEAD_DIM, 1).astype(BF16)
        vc = qkv[:, Q_DIM + KV_DIM + c * LANES:Q_DIM + KV_DIM + (c + 1) * LANES]
        for n in range(tm // BLOCK):
            vt_ref[c * LANES:(c + 1) * LANES, n * BLOCK:(n + 1) * BLOCK] = (
                vc[n * BLOCK:(n + 1) * BLOCK].T.astype(BF16))


def _qkv(x, gains, k_gain, w_qkv, j, tables, b, s):
    t = x.shape[0]
    tm = QKV_TM
    tiles_per_seq = s // tm
    half = HEAD_DIM // 2
    blocks = (_nbytes((tm, D_MODEL), F32) + _nbytes((D_MODEL, QKV_DIM), BF16) + 3 * _nbytes((tm, LANES), F32)
              + 2 * _nbytes((half, tm), F32) + _nbytes((tm, Q_DIM + K_PAD_DIM + KV_DIM), BF16)
              + _nbytes((1, D_MODEL), F32))
    tab_row = pl.BlockSpec((tm, LANES), lambda i: (i % tiles_per_seq, 0))
    tab_col = pl.BlockSpec((half, tm), lambda i: (0, i % tiles_per_seq))
    return pl.pallas_call(
        _qkv_body,
        out_shape=(jax.ShapeDtypeStruct((t // BLOCK, N_KV_HEADS, HEAD_DIM, GROUP * BLOCK), BF16),
                   jax.ShapeDtypeStruct((t, K_PAD_DIM), BF16),
                   jax.ShapeDtypeStruct((b, KV_DIM, s), BF16)),
        grid=(t // tm,),
        in_specs=[
            pl.BlockSpec((tm, D_MODEL), lambda i: (i, 0)),
            _row_spec(k_gain),
            _mat_spec(j, D_MODEL, QKV_DIM),
            tab_row, tab_row, tab_row, tab_col, tab_col,
        ],
        out_specs=(pl.BlockSpec((tm // BLOCK, N_KV_HEADS, HEAD_DIM, GROUP * BLOCK), lambda i: (i, 0, 0, 0)),
                   pl.BlockSpec((tm, K_PAD_DIM), lambda i: (i, 0)),
                   pl.BlockSpec((None, KV_DIM, tm), lambda i: (i // tiles_per_seq, 0, i % tiles_per_seq))),
        compiler_params=pltpu.CompilerParams(
            dimension_semantics=("parallel",),
            vmem_limit_bytes=_vmem_limit(blocks, 0)),
        name="qkv",
    )(x, gains, w_qkv, *tables)


def _attn_body(sink_ref, x_ref, qt_ref, kprev_ref, k_ref, knext_ref, vprev_ref, v_ref, vnext_ref, wo_ref, g_ref,
               o_ref, kx_ref, vx_ref, oh_ref, *, layer):
    i = pl.program_id(1)
    tq = x_ref.shape[1]
    blocks_per_tile = tq // BLOCK
    n_blocks = pl.num_programs(1) * blocks_per_tile
    kx_ref[0:BLOCK] = kprev_ref[0]
    kx_ref[BLOCK:BLOCK + tq] = k_ref[0]
    kx_ref[BLOCK + tq:] = knext_ref[0]
    vx_ref[:, 0:BLOCK] = vprev_ref[0]
    vx_ref[:, BLOCK:BLOCK + tq] = v_ref[0]
    vx_ref[:, BLOCK + tq:] = vnext_ref[0]

    log2e = math.log2(math.e)
    scale = HEAD_DIM ** -0.5 * log2e
    key = lax.broadcasted_iota(jnp.int32, (BLOCK, BLOCK), 0)
    qry = lax.broadcasted_iota(jnp.int32, (BLOCK, BLOCK), 1)

    for n in range(blocks_per_tile):
        gb = i * blocks_per_tile + n
        off_prev = jnp.where(gb == 0, BLOCK, BLOCK - WINDOW)
        off_next = jnp.where(gb == n_blocks - 1, BLOCK, BLOCK - WINDOW)
        mask_prev = jnp.concatenate([key >= qry + off_prev] * GROUP, axis=1)
        mask_next = jnp.concatenate([key <= qry - off_next] * GROUP, axis=1)
        rows = slice(n * BLOCK, (n + 3) * BLOCK)
        for kv in range(N_KV_HEADS):
            kb = kx_ref[rows, kv * LANES:kv * LANES + HEAD_DIM]
            t = jnp.dot(kb, qt_ref[n, kv], preferred_element_type=F32) * scale
            t_prev = jnp.where(mask_prev, t[0:BLOCK], NEG_INF)
            t_own = t[BLOCK:2 * BLOCK]
            t_next = jnp.where(mask_next, t[2 * BLOCK:], NEG_INF)
            sink = jnp.concatenate(
                [jnp.full((1, BLOCK), sink_ref[layer, kv * GROUP + hg] * log2e, F32) for hg in range(GROUP)], axis=1)
            m = jnp.maximum(jnp.maximum(t_prev, t_own), t_next)
            m = jnp.maximum(jnp.max(m, axis=0, keepdims=True), sink)
            e_prev, e_own, e_next = jnp.exp2(t_prev - m), jnp.exp2(t_own - m), jnp.exp2(t_next - m)
            denom = jnp.sum(e_prev + e_own + e_next, axis=0, keepdims=True) + jnp.exp2(sink - m)
            p = jnp.concatenate([e_prev, e_own, e_next], axis=0).astype(BF16)
            vb = vx_ref[kv * HEAD_DIM:(kv + 1) * HEAD_DIM, rows]
            ot = jnp.dot(vb, p, preferred_element_type=F32) * (1.0 / denom)
            for pair in range(GROUP // 2):
                two = jnp.concatenate([ot[:, (2 * pair) * BLOCK:(2 * pair + 1) * BLOCK],
                                       ot[:, (2 * pair + 1) * BLOCK:(2 * pair + 2) * BLOCK]], axis=0)
                col = (kv * GROUP + 2 * pair) * HEAD_DIM
                oh_ref[n * BLOCK:(n + 1) * BLOCK, col:col + 2 * HEAD_DIM] = two.T.astype(BF16)

    att = jnp.dot(oh_ref[...], wo_ref[...], preferred_element_type=F32)
    o_ref[0] = x_ref[0] + _rms(att, g_ref[...])


def _attn(x, qt, kp, vt, gains, k_gain, w_o, sink, j):
    b, s, _ = x.shape
    tq = ATT_TQ
    blocks_per_tile = tq // BLOCK
    tiles_per_seq = s // tq
    n_blocks = s // BLOCK
    main = lambda bi, i: (bi, i, 0)
    prev_blk = lambda i: jnp.maximum(i * blocks_per_tile - 1, 0)
    next_blk = lambda i: jnp.minimum((i + 1) * blocks_per_tile, n_blocks - 1)
    kp = kp.reshape(b, s, K_PAD_DIM)
    blocks = (2 * _nbytes((tq, D_MODEL), F32) + _nbytes((tq, Q_DIM), BF16)
              + _nbytes((tq + 2 * BLOCK, K_PAD_DIM + KV_DIM), BF16) + _nbytes((Q_DIM, D_MODEL), BF16)
              + _nbytes((1, D_MODEL), F32))
    scratch = _nbytes((tq + 2 * BLOCK, K_PAD_DIM + KV_DIM), BF16) + _nbytes((tq, Q_DIM), BF16)
    return pl.pallas_call(
        functools.partial(_attn_body, layer=j),
        out_shape=jax.ShapeDtypeStruct((b, s, D_MODEL), F32),
        grid=(b, tiles_per_seq),
        in_specs=[
            pl.BlockSpec(memory_space=pltpu.SMEM),
            pl.BlockSpec((1, tq, D_MODEL), main),
            pl.BlockSpec((blocks_per_tile, N_KV_HEADS, HEAD_DIM, GROUP * BLOCK),
                         lambda bi, i: (bi * tiles_per_seq + i, 0, 0, 0)),
            pl.BlockSpec((1, BLOCK, K_PAD_DIM), lambda bi, i: (bi, prev_blk(i), 0)),
            pl.BlockSpec((1, tq, K_PAD_DIM), main),
            pl.BlockSpec((1, BLOCK, K_PAD_DIM), lambda bi, i: (bi, next_blk(i), 0)),
            pl.BlockSpec((1, KV_DIM, BLOCK), lambda bi, i: (bi, 0, prev_blk(i))),
            pl.BlockSpec((1, KV_DIM, tq), lambda bi, i: (bi, 0, i)),
            pl.BlockSpec((1, KV_DIM, BLOCK), lambda bi, i: (bi, 0, next_blk(i))),
            _mat_spec(j, Q_DIM, D_MODEL),
            _row_spec(k_gain),
        ],
        out_specs=pl.BlockSpec((1, tq, D_MODEL), main),
        scratch_shapes=[pltpu.VMEM((tq + 2 * BLOCK, K_PAD_DIM), BF16), pltpu.VMEM((KV_DIM, tq + 2 * BLOCK), BF16),
                        pltpu.VMEM((tq, Q_DIM), BF16)],
        compiler_params=pltpu.CompilerParams(
            dimension_semantics=("parallel", "arbitrary"),
            vmem_limit_bytes=_vmem_limit(blocks, scratch)),
        name="attn",
    )(sink, x, qt, kp, kp, kp, vt, vt, vt, w_o, gains)


def _trunk(x, p):
    b, s, _ = x.shape
    t = b * s
    gains = p["gains"]
    x = x.reshape(t, D_MODEL)

    def ffn(x, layer, which):
        k_pre = layer * N_SUBLAYER_NORMS + 4 * which
        return _ffn(x, gains, k_pre, p["wg"], p["wu"], p["wd"], layer, which)

    x = ffn(x, 0, 0)
    h = _glu(x, gains, 2, p["w_pw1"], p["b_pw1"], 0)
    x = _convmix(x.reshape(b, s, D_MODEL), h.reshape(b, s, D_MODEL), gains, 3, p["w_dw"], p["b_dw"], p["ln_g"],
                 p["ln_b"], p["w_pw2"], p["b_pw2"], 0).reshape(t, D_MODEL)
    x = ffn(x, 0, 1)

    x = ffn(x, 1, 0)
    qt, kp, vt = _qkv(x, gains, N_SUBLAYER_NORMS + 2, p["w_qkv"], 0, _rope_tables(s), b, s)
    x = _attn(x.reshape(b, s, D_MODEL), qt, kp, vt, gains, N_SUBLAYER_NORMS + 3, p["w_o"], p["sink"],
              0).reshape(t, D_MODEL)
    x = ffn(x, 1, 1)
    return x.reshape(b, s, D_MODEL)


def kernel(x_prompt, x_sample, norm_g, ffn_w_gate, ffn_w_up, ffn_w_down, conv_w_pw1, conv_b_pw1, conv_w_dw,
           conv_b_dw, conv_ln_g, conv_ln_b, conv_w_pw2, conv_b_pw2, attn_w_qkv, attn_w_o, attn_sink):
    rows = lambda a: a.reshape(-1, 1, a.shape[-1])
    p = {
        "gains": rows(norm_g),
        "wg": ffn_w_gate.astype(BF16), "wu": ffn_w_up.astype(BF16), "wd": ffn_w_down.astype(BF16),
        "w_pw1": conv_w_pw1.astype(BF16), "b_pw1": rows(conv_b_pw1),
        "w_dw": conv_w_dw, "b_dw": rows(conv_b_dw), "ln_g": rows(conv_ln_g), "ln_b": rows(conv_ln_b),
        "w_pw2": conv_w_pw2.astype(BF16), "b_pw2": rows(conv_b_pw2),
        "w_qkv": attn_w_qkv.astype(BF16), "w_o": attn_w_o.astype(BF16), "sink": attn_sink,
    }
    return _trunk(x_prompt, p), _trunk(x_sample, p)
```

```python
import functools
import math

import jax
import jax.numpy as jnp
from jax import lax
from jax.experimental import pallas as pl
from jax.experimental.pallas import tpu as pltpu

F32 = jnp.float32
BF16 = jnp.bfloat16

D_MODEL = 1024
D_FF = 4096
CONV_KERNEL = 31
CONV_HALF = CONV_KERNEL // 2
HEAD_DIM = 64
N_HEADS = 16
N_KV_HEADS = 4
GROUP = N_HEADS // N_KV_HEADS
Q_DIM = N_HEADS * HEAD_DIM
KV_DIM = N_KV_HEADS * HEAD_DIM
QKV_DIM = Q_DIM + 2 * KV_DIM
WINDOW = 128
BLOCK = 128
ROPE_THETA = 10000.0
EPS = 1e-6
NEG_INF = -1e30
N_SUBLAYER_NORMS = 6

LANES = 128
SUBLANES = 8
HALO_ROWS = 16
K_PAD_DIM = N_KV_HEADS * LANES

FFN_TM = 512
FFN_TF = 512
GLU_TM = 512
CONV_TM = 512
CONV_ROWS = 128
QKV_TM = 512
ATT_TQ = 512
ATT_UNROLL = 8

VMEM_TEMP_BYTES = 16 * 1024 * 1024


def _vmem_limit(pipelined_bytes, scratch_bytes):
    return int(2 * pipelined_bytes + scratch_bytes + VMEM_TEMP_BYTES)


def _nbytes(shape, dtype):
    return math.prod(shape) * jnp.dtype(dtype).itemsize


def _rms(x, g):
    ms = jnp.mean(x * x, axis=-1, keepdims=True)
    return x * lax.rsqrt(ms + EPS) * g


def _sigmoid(x):
    return 1.0 / (1.0 + jnp.exp(-x))


def _row_spec(k, width=D_MODEL):
    return pl.BlockSpec((None, 1, width), lambda *_: (k, 0, 0))


def _mat_spec(k, rows, cols):
    return pl.BlockSpec((None, rows, cols), lambda *_: (k, 0, 0))


def _ffn_body(x_ref, gpre_ref, gpost_ref, wg_ref, wu_ref, wd_ref, o_ref):
    x = x_ref[...]
    xn = _rms(x, gpre_ref[...]).astype(BF16)
    acc = None
    for c in range(D_FF // FFN_TF):
        cols = slice(c * FFN_TF, (c + 1) * FFN_TF)
        gate = jnp.dot(xn, wg_ref[:, cols], preferred_element_type=F32)
        up = jnp.dot(xn, wu_ref[:, cols], preferred_element_type=F32)
        h = (gate * _sigmoid(gate) * up).astype(BF16)
        part = jnp.dot(h, wd_ref[cols, :], preferred_element_type=F32)
        acc = part if acc is None else acc + part
    o_ref[...] = x + 0.5 * _rms(acc, gpost_ref[...])


def _ffn(x, gains, k_pre, wg, wu, wd, layer, which):
    t = x.shape[0]
    tm = FFN_TM
    resident = pl.Buffered(1)
    weights = 3 * _nbytes((D_MODEL, D_FF), BF16)
    blocks = 2 * _nbytes((tm, D_MODEL), F32) + 2 * _nbytes((1, D_MODEL), F32)
    return pl.pallas_call(
        _ffn_body,
        out_shape=jax.ShapeDtypeStruct((t, D_MODEL), F32),
        grid=(t // tm,),
        in_specs=[
            pl.BlockSpec((tm, D_MODEL), lambda i: (i, 0)),
            _row_spec(k_pre),
            _row_spec(k_pre + 1),
            pl.BlockSpec((None, None, D_MODEL, D_FF), lambda i: (layer, which, 0, 0), pipeline_mode=resident),
            pl.BlockSpec((None, None, D_MODEL, D_FF), lambda i: (layer, which, 0, 0), pipeline_mode=resident),
            pl.BlockSpec((None, None, D_FF, D_MODEL), lambda i: (layer, which, 0, 0), pipeline_mode=resident),
        ],
        out_specs=pl.BlockSpec((tm, D_MODEL), lambda i: (i, 0)),
        compiler_params=pltpu.CompilerParams(
            dimension_semantics=("parallel",),
            vmem_limit_bytes=_vmem_limit(blocks, weights)),
        name="ffn",
    )(x, gains, gains, wg, wu, wd)


def _glu_body(x_ref, g_ref, w_ref, b_ref, o_ref):
    hn = _rms(x_ref[...], g_ref[...]).astype(BF16)
    h = jnp.dot(hn, w_ref[...], preferred_element_type=F32) + b_ref[...]
    o_ref[...] = h[:, :D_MODEL] * _sigmoid(h[:, D_MODEL:])


def _glu(x, gains, k_gain, w_pw1, b_pw1, j):
    t = x.shape[0]
    tm = GLU_TM
    blocks = (2 * _nbytes((tm, D_MODEL), F32) + _nbytes((D_MODEL, 2 * D_MODEL), BF16)
              + 3 * _nbytes((1, D_MODEL), F32))
    return pl.pallas_call(
        _glu_body,
        out_shape=jax.ShapeDtypeStruct((t, D_MODEL), F32),
        grid=(t // tm,),
        in_specs=[
            pl.BlockSpec((tm, D_MODEL), lambda i: (i, 0)),
            _row_spec(k_gain),
            _mat_spec(j, D_MODEL, 2 * D_MODEL),
            _row_spec(j, 2 * D_MODEL),
        ],
        out_specs=pl.BlockSpec((tm, D_MODEL), lambda i: (i, 0)),
        compiler_params=pltpu.CompilerParams(
            dimension_semantics=("parallel",),
            vmem_limit_bytes=_vmem_limit(blocks, 0)),
        name="glu",
    )(x, gains, w_pw1, b_pw1)


def _convmix_body(x_ref, h_ref, hprev_ref, hnext_ref, wdw_ref, bdw_ref, lng_ref, lnb_ref,
                  w2_ref, b2_ref, g_ref, o_ref, hp_ref, c_ref):
    i = pl.program_id(1)
    tm = h_ref.shape[1]
    pieces = ((0, jnp.where(i > 0, hprev_ref[0], 0.0)),
              (HALO_ROWS, h_ref[0]),
              (HALO_ROWS + tm, jnp.where(i < pl.num_programs(1) - 1, hnext_ref[0], 0.0)))
    for row0, piece in pieces:
        for c in range(D_MODEL // LANES):
            hp_ref[c // 2, pl.ds(2 * row0 + c % 2, piece.shape[0], stride=2), :] = piece[:, c * LANES:(c + 1) * LANES]

    base = HALO_ROWS - CONV_HALF
    n_tiles = -(-(base + CONV_KERNEL) // SUBLANES)
    win = CONV_ROWS + (n_tiles - 1) * SUBLANES
    for c in range(D_MODEL // LANES):
        cols = slice(c * LANES, (c + 1) * LANES)

        def rows(r, carry, c=c, cols=cols):
            r0 = pl.multiple_of(r * CONV_ROWS, CONV_ROWS)
            acc = None
            for shift in range(SUBLANES):
                vs = hp_ref[c // 2, pl.ds(2 * (r0 + shift) + c % 2, win, stride=2), :]
                for tile in range(n_tiles):
                    k = tile * SUBLANES + shift - base
                    if 0 <= k < CONV_KERNEL:
                        term = vs[tile * SUBLANES:tile * SUBLANES + CONV_ROWS] * wdw_ref[k:k + 1, cols]
                        acc = term if acc is None else acc + term
            c_ref[pl.ds(r0, CONV_ROWS), cols] = acc
            return carry

        lax.fori_loop(0, tm // CONV_ROWS, rows, 0)

    y = c_ref[...] + bdw_ref[...]
    mu = jnp.mean(y, axis=-1, keepdims=True)
    yc = y - mu
    var = jnp.mean(yc * yc, axis=-1, keepdims=True)
    y = yc * lax.rsqrt(var + EPS) * lng_ref[...] + lnb_ref[...]
    y = (y * _sigmoid(y)).astype(BF16)
    h = jnp.dot(y, w2_ref[...], preferred_element_type=F32) + b2_ref[...]
    o_ref[0] = x_ref[0] + _rms(h, g_ref[...])


def _convmix(x, h, gains, k_gain, w_dw, b_dw, ln_g, ln_b, w_pw2, b_pw2, j):
    b, s, _ = x.shape
    tm = CONV_TM
    halo_per_tile = tm // HALO_ROWS
    n_halo = s // HALO_ROWS
    blocks = (3 * _nbytes((tm, D_MODEL), F32) + 2 * _nbytes((HALO_ROWS, D_MODEL), F32)
              + _nbytes((CONV_KERNEL, D_MODEL), F32) + _nbytes((D_MODEL, D_MODEL), BF16)
              + 6 * _nbytes((1, D_MODEL), F32))
    scratch = _nbytes((tm + 2 * HALO_ROWS, D_MODEL), F32) + _nbytes((tm, D_MODEL), F32)
    return pl.pallas_call(
        _convmix_body,
        out_shape=jax.ShapeDtypeStruct((b, s, D_MODEL), F32),
        grid=(b, s // tm),
        in_specs=[
            pl.BlockSpec((1, tm, D_MODEL), lambda bi, i: (bi, i, 0)),
            pl.BlockSpec((1, tm, D_MODEL), lambda bi, i: (bi, i, 0)),
            pl.BlockSpec((1, HALO_ROWS, D_MODEL),
                         lambda bi, i: (bi, jnp.maximum(i * halo_per_tile - 1, 0), 0)),
            pl.BlockSpec((1, HALO_ROWS, D_MODEL),
                         lambda bi, i: (bi, jnp.minimum((i + 1) * halo_per_tile, n_halo - 1), 0)),
            _mat_spec(j, CONV_KERNEL, D_MODEL),
            _row_spec(j), _row_spec(j), _row_spec(j),
            _mat_spec(j, D_MODEL, D_MODEL),
            _row_spec(j),
            _row_spec(k_gain),
        ],
        out_specs=pl.BlockSpec((1, tm, D_MODEL), lambda bi, i: (bi, i, 0)),
        scratch_shapes=[pltpu.VMEM((D_MODEL // (2 * LANES), 2 * (tm + 2 * HALO_ROWS), LANES), F32),
                        pltpu.VMEM((tm, D_MODEL), F32)],
        compiler_params=pltpu.CompilerParams(
            dimension_semantics=("parallel", "arbitrary"),
            vmem_limit_bytes=_vmem_limit(blocks, scratch)),
        name="convmix",
    )(x, h, h, h, w_dw, b_dw, ln_g, ln_b, w_pw2, b_pw2, gains)


def _rope_tables(s):
    half = HEAD_DIM // 2
    inv = ROPE_THETA ** (-jnp.arange(half, dtype=F32) / half)
    ang = jnp.arange(s).astype(F32)[:, None] * inv[None, :]
    cos, sin, zero = jnp.cos(ang), jnp.sin(ang), jnp.zeros_like(ang)
    reps = LANES // HEAD_DIM
    cos_row = jnp.tile(jnp.concatenate([cos, cos], axis=-1), (1, reps))
    sin_lo = jnp.tile(jnp.concatenate([-sin, zero], axis=-1), (1, reps))
    sin_hi = jnp.tile(jnp.concatenate([zero, sin], axis=-1), (1, reps))
    return cos_row, sin_lo, sin_hi, cos.T, sin.T


def _qkv_body(x_ref, g_ref, w_ref, cos_ref, slo_ref, shi_ref, cost_ref, sint_ref, qt_ref, kp_ref, vt_ref):
    tm = x_ref.shape[0]
    hn = _rms(x_ref[...], g_ref[...]).astype(BF16)
    qkv = jnp.dot(hn, w_ref[...], preferred_element_type=F32)
    half = HEAD_DIM // 2
    heads_per_group = LANES // HEAD_DIM

    for c in range(Q_DIM // LANES):
        qc = qkv[:, c * LANES:(c + 1) * LANES]
        for n in range(tm // BLOCK):
            qct = qc[n * BLOCK:(n + 1) * BLOCK].T
            cos = cost_ref[:, n * BLOCK:(n + 1) * BLOCK]
            sin = sint_ref[:, n * BLOCK:(n + 1) * BLOCK]
            for hl in range(heads_per_group):
                head = c * heads_per_group + hl
                kv, hg = head // GROUP, head % GROUP
                x1 = qct[hl * HEAD_DIM:hl * HEAD_DIM + half]
                x2 = qct[hl * HEAD_DIM + half:(hl + 1) * HEAD_DIM]
                roped = jnp.concatenate([x1 * cos - x2 * sin, x2 * cos + x1 * sin], axis=0)
                qt_ref[n, kv, :, hg * BLOCK:(hg + 1) * BLOCK] = roped.astype(BF16)

    cos, slo, shi = cos_ref[...], slo_ref[...], shi_ref[...]
    for c in range(KV_DIM // LANES):
        kc = qkv[:, Q_DIM + c * LANES:Q_DIM + (c + 1) * LANES]
        kc = kc * cos + pltpu.roll(kc, LANES - half, 1) * slo + pltpu.roll(kc, half, 1) * shi
        kp_ref[2 * c] = kc.astype(BF16)
        kp_ref[2 * c + 1] = pltpu.roll(kc, HEAD_DIM, 1).astype(BF16)
        vc = qkv[:, Q_DIM + KV_DIM + c * LANES:Q_DIM + KV_DIM + (c + 1) * LANES]
        for n in range(tm // BLOCK):
            vt_ref[c * LANES:(c + 1) * LANES, n * BLOCK:(n + 1) * BLOCK] = (
                vc[n * BLOCK:(n + 1) * BLOCK].T.astype(BF16))


def _qkv(x, gains, k_gain, w_qkv, j, tables, b, s):
    t = x.shape[0]
    tm = QKV_TM
    tiles_per_seq = s // tm
    half = HEAD_DIM // 2
    blocks = (_nbytes((tm, D_MODEL), F32) + _nbytes((D_MODEL, QKV_DIM), BF16) + 3 * _nbytes((tm, LANES), F32)
              + 2 * _nbytes((half, tm), F32) + _nbytes((tm, Q_DIM + K_PAD_DIM + KV_DIM), BF16)
              + _nbytes((1, D_MODEL), F32))
    tab_row = pl.BlockSpec((tm, LANES), lambda i: (i % tiles_per_seq, 0))
    tab_col = pl.BlockSpec((half, tm), lambda i: (0, i % tiles_per_seq))
    return pl.pallas_call(
        _qkv_body,
        out_shape=(jax.ShapeDtypeStruct((t // BLOCK, N_KV_HEADS, HEAD_DIM, GROUP * BLOCK), BF16),
                   jax.ShapeDtypeStruct((b, N_KV_HEADS, s, LANES), BF16),
                   jax.ShapeDtypeStruct((b, KV_DIM, s), BF16)),
        grid=(t // tm,),
        in_specs=[
            pl.BlockSpec((tm, D_MODEL), lambda i: (i, 0)),
            _row_spec(k_gain),
            _mat_spec(j, D_MODEL, QKV_DIM),
            tab_row, tab_row, tab_row, tab_col, tab_col,
        ],
        out_specs=(pl.BlockSpec((tm // BLOCK, N_KV_HEADS, HEAD_DIM, GROUP * BLOCK), lambda i: (i, 0, 0, 0)),
                   pl.BlockSpec((None, N_KV_HEADS, tm, LANES),
                                lambda i: (i // tiles_per_seq, 0, i % tiles_per_seq, 0)),
                   pl.BlockSpec((None, KV_DIM, tm), lambda i: (i // tiles_per_seq, 0, i % tiles_per_seq))),
        compiler_params=pltpu.CompilerParams(
            dimension_semantics=("parallel",),
            vmem_limit_bytes=_vmem_limit(blocks, 0)),
        name="qkv",
    )(x, gains, w_qkv, *tables)


def _attn_body(sink_ref, x_ref, qt_ref, kprev_ref, k_ref, knext_ref, vprev_ref, v_ref, vnext_ref, wo_ref, g_ref,
               o_ref, kx_ref, vw_ref, oh_ref, ta_ref, tb_ref, *, layer):
    i = pl.program_id(1)
    tq = x_ref.shape[1]
    blocks_per_tile = tq // BLOCK
    n_chains = blocks_per_tile * N_KV_HEADS
    n_blocks = pl.num_programs(1) * blocks_per_tile
    kx_ref[:, 0:BLOCK] = kprev_ref[0]
    kx_ref[:, BLOCK:BLOCK + tq] = k_ref[0]
    kx_ref[:, BLOCK + tq:] = knext_ref[0]
    vx = jnp.concatenate([vprev_ref[0], v_ref[0], vnext_ref[0]], axis=1)
    for n in range(blocks_per_tile):
        vw_ref[n] = vx[:, n * BLOCK:(n + 3) * BLOCK]

    log2e = math.log2(math.e)
    scale = HEAD_DIM ** -0.5 * log2e
    key = lax.broadcasted_iota(jnp.int32, (BLOCK, BLOCK), 0)
    qry = lax.broadcasted_iota(jnp.int32, (BLOCK, BLOCK), 1)

    def scores(c, t_ref):
        n, kv = c // N_KV_HEADS, c % N_KV_HEADS
        kb = kx_ref[kv, pl.ds(pl.multiple_of(n * BLOCK, BLOCK), 3 * BLOCK), 0:HEAD_DIM]
        t_ref[...] = jnp.dot(kb, qt_ref[n, kv], preferred_element_type=F32) * scale

    def finish(c, t_ref):
        n, kv = c // N_KV_HEADS, c % N_KV_HEADS
        gb = i * blocks_per_tile + n
        off_prev = jnp.where(gb == 0, BLOCK, BLOCK - WINDOW)
        off_next = jnp.where(gb == n_blocks - 1, BLOCK, BLOCK - WINDOW)
        mask_prev = jnp.concatenate([key >= qry + off_prev] * GROUP, axis=1)
        mask_next = jnp.concatenate([key <= qry - off_next] * GROUP, axis=1)
        t_prev = jnp.where(mask_prev, t_ref[0:BLOCK], NEG_INF)
        t_own = t_ref[BLOCK:2 * BLOCK]
        t_next = jnp.where(mask_next, t_ref[2 * BLOCK:], NEG_INF)
        sink = jnp.concatenate(
            [jnp.full((1, BLOCK), sink_ref[layer, kv * GROUP + hg] * log2e, F32) for hg in range(GROUP)], axis=1)
        m = jnp.maximum(jnp.maximum(t_prev, t_own), t_next)
        m = jnp.maximum(jnp.max(m, axis=0, keepdims=True), sink)
        e_prev, e_own, e_next = jnp.exp2(t_prev - m), jnp.exp2(t_own - m), jnp.exp2(t_next - m)
        denom = jnp.sum(e_prev + e_own + e_next, axis=0, keepdims=True) + jnp.exp2(sink - m)
        p = jnp.concatenate([e_prev, e_own, e_next], axis=0).astype(BF16)
        vb = vw_ref[n, pl.ds(pl.multiple_of(kv * HEAD_DIM, HEAD_DIM), HEAD_DIM), :]
        ot = jnp.dot(vb, p, preferred_element_type=F32) * (1.0 / denom)
        r0 = pl.multiple_of(n * BLOCK, BLOCK)
        for pair in range(GROUP // 2):
            two = jnp.concatenate([ot[:, (2 * pair) * BLOCK:(2 * pair + 1) * BLOCK],
                                   ot[:, (2 * pair + 1) * BLOCK:(2 * pair + 2) * BLOCK]], axis=0)
            oh_ref[kv * (GROUP // 2) + pair, pl.ds(r0, BLOCK), :] = two.T.astype(BF16)

    t_bufs = (ta_ref, tb_ref)
    scores(0, t_bufs[0])

    def step(j, carry):
        first = ATT_UNROLL * j
        for u in range(ATT_UNROLL):
            scores(jnp.minimum(first + u + 1, n_chains - 1), t_bufs[(u + 1) % 2])
            finish(first + u, t_bufs[u % 2])
        return carry

    lax.fori_loop(0, n_chains // ATT_UNROLL, step, 0)
    oh = jnp.concatenate([oh_ref[pp] for pp in range(N_HEADS // 2)], axis=1)
    att = jnp.dot(oh, wo_ref[...], preferred_element_type=F32)
    o_ref[0] = x_ref[0] + _rms(att, g_ref[...])


def _attn(x, qt, kp, vt, gains, k_gain, w_o, sink, j):
    b, s, _ = x.shape
    tq = ATT_TQ
    blocks_per_tile = tq // BLOCK
    tiles_per_seq = s // tq
    n_blocks = s // BLOCK
    main = lambda bi, i: (bi, i, 0)
    prev_blk = lambda i: jnp.maximum(i * blocks_per_tile - 1, 0)
    next_blk = lambda i: jnp.minimum((i + 1) * blocks_per_tile, n_blocks - 1)
    kx_shape = (N_KV_HEADS, tq + 2 * BLOCK, LANES)
    vw_shape = (blocks_per_tile, KV_DIM, 3 * BLOCK)
    oh_shape = (N_HEADS // 2, tq, LANES)
    t_shape = (3 * BLOCK, GROUP * BLOCK)
    blocks = (2 * _nbytes((tq, D_MODEL), F32) + _nbytes((tq, Q_DIM), BF16) + _nbytes(kx_shape, BF16)
              + _nbytes((KV_DIM, tq + 2 * BLOCK), BF16) + _nbytes((Q_DIM, D_MODEL), BF16) + _nbytes((1, D_MODEL), F32))
    scratch = (_nbytes(kx_shape, BF16) + _nbytes(vw_shape, BF16) + _nbytes(oh_shape, BF16)
               + 2 * _nbytes(t_shape, F32))
    return pl.pallas_call(
        functools.partial(_attn_body, layer=j),
        out_shape=jax.ShapeDtypeStruct((b, s, D_MODEL), F32),
        grid=(b, tiles_per_seq),
        in_specs=[
            pl.BlockSpec(memory_space=pltpu.SMEM),
            pl.BlockSpec((1, tq, D_MODEL), main),
            pl.BlockSpec((blocks_per_tile, N_KV_HEADS, HEAD_DIM, GROUP * BLOCK),
                         lambda bi, i: (bi * tiles_per_seq + i, 0, 0, 0)),
            pl.BlockSpec((1, N_KV_HEADS, BLOCK, LANES), lambda bi, i: (bi, 0, prev_blk(i), 0)),
            pl.BlockSpec((1, N_KV_HEADS, tq, LANES), lambda bi, i: (bi, 0, i, 0)),
            pl.BlockSpec((1, N_KV_HEADS, BLOCK, LANES), lambda bi, i: (bi, 0, next_blk(i), 0)),
            pl.BlockSpec((1, KV_DIM, BLOCK), lambda bi, i: (bi, 0, prev_blk(i))),
            pl.BlockSpec((1, KV_DIM, tq), lambda bi, i: (bi, 0, i)),
            pl.BlockSpec((1, KV_DIM, BLOCK), lambda bi, i: (bi, 0, next_blk(i))),
            _mat_spec(j, Q_DIM, D_MODEL),
            _row_spec(k_gain),
        ],
        out_specs=pl.BlockSpec((1, tq, D_MODEL), main),
        scratch_shapes=[pltpu.VMEM(kx_shape, BF16), pltpu.VMEM(vw_shape, BF16), pltpu.VMEM(oh_shape, BF16),
                        pltpu.VMEM(t_shape, F32), pltpu.VMEM(t_shape, F32)],
        compiler_params=pltpu.CompilerParams(
            dimension_semantics=("parallel", "arbitrary"),
            vmem_limit_bytes=_vmem_limit(blocks, scratch)),
        name="attn",
    )(sink, x, qt, kp, kp, kp, vt, vt, vt, w_o, gains)


def _trunk(x, p):
    b, s, _ = x.shape
    t = b * s
    gains = p["gains"]
    x = x.reshape(t, D_MODEL)

    def ffn(x, layer, which):
        k_pre = layer * N_SUBLAYER_NORMS + 4 * which
        return _ffn(x, gains, k_pre, p["wg"], p["wu"], p["wd"], layer, which)

    x = ffn(x, 0, 0)
    h = _glu(x, gains, 2, p["w_pw1"], p["b_pw1"], 0)
    x = _convmix(x.reshape(b, s, D_MODEL), h.reshape(b, s, D_MODEL), gains, 3, p["w_dw"], p["b_dw"], p["ln_g"],
                 p["ln_b"], p["w_pw2"], p["b_pw2"], 0).reshape(t, D_MODEL)
    x = ffn(x, 0, 1)

    x = ffn(x, 1, 0)
    qt, kp, vt = _qkv(x, gains, N_SUBLAYER_NORMS + 2, p["w_qkv"], 0, _rope_tables(s), b, s)
    x = _attn(x.reshape(b, s, D_MODEL), qt, kp, vt, gains, N_SUBLAYER_NORMS + 3, p["w_o"], p["sink"],
              0).reshape(t, D_MODEL)
    x = ffn(x, 1, 1)
    return x.reshape(b, s, D_MODEL)


def kernel(x_prompt, x_sample, norm_g, ffn_w_gate, ffn_w_up, ffn_w_down, conv_w_pw1, conv_b_pw1, conv_w_dw,
           conv_b_dw, conv_ln_g, conv_ln_b, conv_w_pw2, conv_b_pw2, attn_w_qkv, attn_w_o, attn_sink):
    rows = lambda a: a.reshape(-1, 1, a.shape[-1])
    p = {
        "gains": rows(norm_g),
        "wg": ffn_w_gate.astype(BF16), "wu": ffn_w_up.astype(BF16), "wd": ffn_w_down.astype(BF16),
        "w_pw1": conv_w_pw1.astype(BF16), "b_pw1": rows(conv_b_pw1),
        "w_dw": conv_w_dw, "b_dw": rows(conv_b_dw), "ln_g": rows(conv_ln_g), "ln_b": rows(conv_ln_b),
        "w_pw2": conv_w_pw2.astype(BF16), "b_pw2": rows(conv_b_pw2),
        "w_qkv": attn_w_qkv.astype(BF16), "w_o": attn_w_o.astype(BF16), "sink": attn_sink,
    }
    return _trunk(x_prompt, p), _trunk(x_sample, p)
```

```python
import functools
import math

import jax
import jax.numpy as jnp
from jax import lax
from jax.experimental import pallas as pl
from jax.experimental.pallas import tpu as pltpu

F32 = jnp.float32
BF16 = jnp.bfloat16

D_MODEL = 1024
D_FF = 4096
CONV_KERNEL = 31
CONV_HALF = CONV_KERNEL // 2
HEAD_DIM = 64
N_HEADS = 16
N_KV_HEADS = 4
GROUP = N_HEADS // N_KV_HEADS
Q_DIM = N_HEADS * HEAD_DIM
KV_DIM = N_KV_HEADS * HEAD_DIM
QKV_DIM = Q_DIM + 2 * KV_DIM
WINDOW = 128
BLOCK = 128
ROPE_THETA = 10000.0
EPS = 1e-6
NEG_INF = -1e30
N_SUBLAYER_NORMS = 6

LANES = 128
SUBLANES = 8
HALO_ROWS = 16
K_PAD_DIM = N_KV_HEADS * LANES

FFN_TM = 512
FFN_TF = 512
GLU_TM = 512
CONV_TM = 512
CONV_ROWS = 128
QKV_TM = 512
ATT_TQ = 1024
ATT_UNROLL = 32

VMEM_TEMP_BYTES = 16 * 1024 * 1024


def _vmem_limit(pipelined_bytes, scratch_bytes):
    return int(2 * pipelined_bytes + scratch_bytes + VMEM_TEMP_BYTES)


def _nbytes(shape, dtype):
    return math.prod(shape) * jnp.dtype(dtype).itemsize


def _rms(x, g):
    ms = jnp.mean(x * x, axis=-1, keepdims=True)
    return x * lax.rsqrt(ms + EPS) * g


def _sigmoid(x):
    return 1.0 / (1.0 + jnp.exp(-x))


def _row_spec(k, width=D_MODEL):
    return pl.BlockSpec((None, 1, width), lambda *_: (k, 0, 0))


def _mat_spec(k, rows, cols):
    return pl.BlockSpec((None, rows, cols), lambda *_: (k, 0, 0))


def _ffn_body(x_ref, gpre_ref, gpost_ref, wg_ref, wu_ref, wd_ref, o_ref):
    x = x_ref[...]
    xn = _rms(x, gpre_ref[...]).astype(BF16)
    acc = None
    for c in range(D_FF // FFN_TF):
        cols = slice(c * FFN_TF, (c + 1) * FFN_TF)
        gate = jnp.dot(xn, wg_ref[:, cols], preferred_element_type=F32)
        up = jnp.dot(xn, wu_ref[:, cols], preferred_element_type=F32)
        h = (gate * _sigmoid(gate) * up).astype(BF16)
        part = jnp.dot(h, wd_ref[cols, :], preferred_element_type=F32)
        acc = part if acc is None else acc + part
    o_ref[...] = x + _rms(acc, 0.5 * gpost_ref[...])


def _ffn(x, gains, k_pre, wg, wu, wd, layer, which):
    t = x.shape[0]
    tm = FFN_TM
    resident = pl.Buffered(1)
    weights = 3 * _nbytes((D_MODEL, D_FF), BF16)
    blocks = 2 * _nbytes((tm, D_MODEL), F32) + 2 * _nbytes((1, D_MODEL), F32)
    return pl.pallas_call(
        _ffn_body,
        out_shape=jax.ShapeDtypeStruct((t, D_MODEL), F32),
        grid=(t // tm,),
        in_specs=[
            pl.BlockSpec((tm, D_MODEL), lambda i: (i, 0)),
            _row_spec(k_pre),
            _row_spec(k_pre + 1),
            pl.BlockSpec((None, None, D_MODEL, D_FF), lambda i: (layer, which, 0, 0), pipeline_mode=resident),
            pl.BlockSpec((None, None, D_MODEL, D_FF), lambda i: (layer, which, 0, 0), pipeline_mode=resident),
            pl.BlockSpec((None, None, D_FF, D_MODEL), lambda i: (layer, which, 0, 0), pipeline_mode=resident),
        ],
        out_specs=pl.BlockSpec((tm, D_MODEL), lambda i: (i, 0)),
        compiler_params=pltpu.CompilerParams(
            dimension_semantics=("parallel",),
            vmem_limit_bytes=_vmem_limit(blocks, weights)),
        name="ffn",
    )(x, gains, gains, wg, wu, wd)


def _glu_body(x_ref, g_ref, w_ref, b_ref, o_ref):
    hn = _rms(x_ref[...], g_ref[...]).astype(BF16)
    h = jnp.dot(hn, w_ref[...], preferred_element_type=F32) + b_ref[...]
    o_ref[...] = h[:, :D_MODEL] * _sigmoid(h[:, D_MODEL:])


def _glu(x, gains, k_gain, w_pw1, b_pw1, j):
    t = x.shape[0]
    tm = GLU_TM
    blocks = (2 * _nbytes((tm, D_MODEL), F32) + _nbytes((D_MODEL, 2 * D_MODEL), BF16)
              + 3 * _nbytes((1, D_MODEL), F32))
    return pl.pallas_call(
        _glu_body,
        out_shape=jax.ShapeDtypeStruct((t, D_MODEL), F32),
        grid=(t // tm,),
        in_specs=[
            pl.BlockSpec((tm, D_MODEL), lambda i: (i, 0)),
            _row_spec(k_gain),
            _mat_spec(j, D_MODEL, 2 * D_MODEL),
            _row_spec(j, 2 * D_MODEL),
        ],
        out_specs=pl.BlockSpec((tm, D_MODEL), lambda i: (i, 0)),
        compiler_params=pltpu.CompilerParams(
            dimension_semantics=("parallel",),
            vmem_limit_bytes=_vmem_limit(blocks, 0)),
        name="glu",
    )(x, gains, w_pw1, b_pw1)


def _convmix_body(x_ref, h_ref, hprev_ref, hnext_ref, wdw_ref, bdw_ref, lng_ref, lnb_ref,
                  w2_ref, b2_ref, g_ref, o_ref, hp_ref, c_ref):
    i = pl.program_id(1)
    tm = h_ref.shape[1]
    pieces = ((0, jnp.where(i > 0, hprev_ref[0], 0.0)),
              (HALO_ROWS, h_ref[0]),
              (HALO_ROWS + tm, jnp.where(i < pl.num_programs(1) - 1, hnext_ref[0], 0.0)))
    for row0, piece in pieces:
        for c in range(D_MODEL // LANES):
            hp_ref[c // 2, pl.ds(2 * row0 + c % 2, piece.shape[0], stride=2), :] = piece[:, c * LANES:(c + 1) * LANES]

    base = HALO_ROWS - CONV_HALF
    n_tiles = -(-(base + CONV_KERNEL) // SUBLANES)
    win = CONV_ROWS + (n_tiles - 1) * SUBLANES
    for c in range(D_MODEL // LANES):
        cols = slice(c * LANES, (c + 1) * LANES)

        def rows(r, carry, c=c, cols=cols):
            r0 = pl.multiple_of(r * CONV_ROWS, CONV_ROWS)
            acc = None
            for shift in range(SUBLANES):
                vs = hp_ref[c // 2, pl.ds(2 * (r0 + shift) + c % 2, win, stride=2), :]
                for tile in range(n_tiles):
                    k = tile * SUBLANES + shift - base
                    if 0 <= k < CONV_KERNEL:
                        term = vs[tile * SUBLANES:tile * SUBLANES + CONV_ROWS] * wdw_ref[k:k + 1, cols]
                        acc = term if acc is None else acc + term
            c_ref[pl.ds(r0, CONV_ROWS), cols] = acc
            return carry

        lax.fori_loop(0, tm // CONV_ROWS, rows, 0)

    y = c_ref[...] + bdw_ref[...]
    mu = jnp.mean(y, axis=-1, keepdims=True)
    yc = y - mu
    var = jnp.mean(yc * yc, axis=-1, keepdims=True)
    y = yc * lax.rsqrt(var + EPS) * lng_ref[...] + lnb_ref[...]
    y = (y * _sigmoid(y)).astype(BF16)
    h = jnp.dot(y, w2_ref[...], preferred_element_type=F32) + b2_ref[...]
    o_ref[0] = x_ref[0] + _rms(h, g_ref[...])


def _convmix(x, h, gains, k_gain, w_dw, b_dw, ln_g, ln_b, w_pw2, b_pw2, j):
    b, s, _ = x.shape
    tm = CONV_TM
    halo_per_tile = tm // HALO_ROWS
    n_halo = s // HALO_ROWS
    blocks = (3 * _nbytes((tm, D_MODEL), F32) + 2 * _nbytes((HALO_ROWS, D_MODEL), F32)
              + _nbytes((CONV_KERNEL, D_MODEL), F32) + _nbytes((D_MODEL, D_MODEL), BF16)
              + 6 * _nbytes((1, D_MODEL), F32))
    scratch = _nbytes((tm + 2 * HALO_ROWS, D_MODEL), F32) + _nbytes((tm, D_MODEL), F32)
    return pl.pallas_call(
        _convmix_body,
        out_shape=jax.ShapeDtypeStruct((b, s, D_MODEL), F32),
        grid=(b, s // tm),
        in_specs=[
            pl.BlockSpec((1, tm, D_MODEL), lambda bi, i: (bi, i, 0)),
            pl.BlockSpec((1, tm, D_MODEL), lambda bi, i: (bi, i, 0)),
            pl.BlockSpec((1, HALO_ROWS, D_MODEL),
                         lambda bi, i: (bi, jnp.maximum(i * halo_per_tile - 1, 0), 0)),
            pl.BlockSpec((1, HALO_ROWS, D_MODEL),
                         lambda bi, i: (bi, jnp.minimum((i + 1) * halo_per_tile, n_halo - 1), 0)),
            _mat_spec(j, CONV_KERNEL, D_MODEL),
            _row_spec(j), _row_spec(j), _row_spec(j),
            _mat_spec(j, D_MODEL, D_MODEL),
            _row_spec(j),
            _row_spec(k_gain),
        ],
        out_specs=pl.BlockSpec((1, tm, D_MODEL), lambda bi, i: (bi, i, 0)),
        scratch_shapes=[pltpu.VMEM((D_MODEL // (2 * LANES), 2 * (tm + 2 * HALO_ROWS), LANES), F32),
                        pltpu.VMEM((tm, D_MODEL), F32)],
        compiler_params=pltpu.CompilerParams(
            dimension_semantics=("parallel", "arbitrary"),
            vmem_limit_bytes=_vmem_limit(blocks, scratch)),
        name="convmix",
    )(x, h, h, h, w_dw, b_dw, ln_g, ln_b, w_pw2, b_pw2, gains)


def _rope_tables(s):
    half = HEAD_DIM // 2
    inv = ROPE_THETA ** (-jnp.arange(half, dtype=F32) / half)
    ang = jnp.arange(s).astype(F32)[:, None] * inv[None, :]
    cos, sin, zero = jnp.cos(ang), jnp.sin(ang), jnp.zeros_like(ang)
    reps = LANES // HEAD_DIM
    cos_row = jnp.tile(jnp.concatenate([cos, cos], axis=-1), (1, reps))
    sin_lo = jnp.tile(jnp.concatenate([-sin, zero], axis=-1), (1, reps))
    sin_hi = jnp.tile(jnp.concatenate([zero, sin], axis=-1), (1, reps))
    return cos_row, sin_lo, sin_hi, cos.T, sin.T


def _qkv_body(x_ref, g_ref, w_ref, cos_ref, slo_ref, shi_ref, cost_ref, sint_ref, qt_ref, kp_ref, vt_ref):
    tm = x_ref.shape[0]
    hn = _rms(x_ref[...], g_ref[...]).astype(BF16)
    qkv = jnp.dot(hn, w_ref[...], preferred_element_type=F32)
    half = HEAD_DIM // 2
    heads_per_group = LANES // HEAD_DIM

    for c in range(Q_DIM // LANES):
        qc = qkv[:, c * LANES:(c + 1) * LANES]
        for n in range(tm // BLOCK):
            qct = qc[n * BLOCK:(n + 1) * BLOCK].T
            cos = cost_ref[:, n * BLOCK:(n + 1) * BLOCK]
            sin = sint_ref[:, n * BLOCK:(n + 1) * BLOCK]
            for hl in range(heads_per_group):
                head = c * heads_per_group + hl
                kv, hg = head // GROUP, head % GROUP
                x1 = qct[hl * HEAD_DIM:hl * HEAD_DIM + half]
                x2 = qct[hl * HEAD_DIM + half:(hl + 1) * HEAD_DIM]
                roped = jnp.concatenate([x1 * cos - x2 * sin, x2 * cos + x1 * sin], axis=0)
                qt_ref[n, kv, :, hg * BLOCK:(hg + 1) * BLOCK] = roped.astype(BF16)

    cos, slo, shi = cos_ref[...], slo_ref[...], shi_ref[...]
    for c in range(KV_DIM // LANES):
        kc = qkv[:, Q_DIM + c * LANES:Q_DIM + (c + 1) * LANES]
        kc = kc * cos + pltpu.roll(kc, LANES - half, 1) * slo + pltpu.roll(kc, half, 1) * shi
        kp_ref[2 * c] = kc.astype(BF16)
        kp_ref[2 * c + 1] = pltpu.roll(kc, HEAD_DIM, 1).astype(BF16)
        vc = qkv[:, Q_DIM + KV_DIM + c * LANES:Q_DIM + KV_DIM + (c + 1) * LANES]
        for n in range(tm // BLOCK):
            vt_ref[c * LANES:(c + 1) * LANES, n * BLOCK:(n + 1) * BLOCK] = (
                vc[n * BLOCK:(n + 1) * BLOCK].T.astype(BF16))


def _qkv(x, gains, k_gain, w_qkv, j, tables, b, s):
    t = x.shape[0]
    tm = QKV_TM
    tiles_per_seq = s // tm
    half = HEAD_DIM // 2
    blocks = (_nbytes((tm, D_MODEL), F32) + _nbytes((D_MODEL, QKV_DIM), BF16) + 3 * _nbytes((tm, LANES), F32)
              + 2 * _nbytes((half, tm), F32) + _nbytes((tm, Q_DIM + K_PAD_DIM + KV_DIM), BF16)
              + _nbytes((1, D_MODEL), F32))
    tab_row = pl.BlockSpec((tm, LANES), lambda i: (i % tiles_per_seq, 0))
    tab_col = pl.BlockSpec((half, tm), lambda i: (0, i % tiles_per_seq))
    return pl.pallas_call(
        _qkv_body,
        out_shape=(jax.ShapeDtypeStruct((t // BLOCK, N_KV_HEADS, HEAD_DIM, GROUP * BLOCK), BF16),
                   jax.ShapeDtypeStruct((b, N_KV_HEADS, s, LANES), BF16),
                   jax.ShapeDtypeStruct((b, KV_DIM, s), BF16)),
        grid=(t // tm,),
        in_specs=[
            pl.BlockSpec((tm, D_MODEL), lambda i: (i, 0)),
            _row_spec(k_gain),
            _mat_spec(j, D_MODEL, QKV_DIM),
            tab_row, tab_row, tab_row, tab_col, tab_col,
        ],
        out_specs=(pl.BlockSpec((tm // BLOCK, N_KV_HEADS, HEAD_DIM, GROUP * BLOCK), lambda i: (i, 0, 0, 0)),
                   pl.BlockSpec((None, N_KV_HEADS, tm, LANES),
                                lambda i: (i // tiles_per_seq, 0, i % tiles_per_seq, 0)),
                   pl.BlockSpec((None, KV_DIM, tm), lambda i: (i // tiles_per_seq, 0, i % tiles_per_seq))),
        compiler_params=pltpu.CompilerParams(
            dimension_semantics=("parallel",),
            vmem_limit_bytes=_vmem_limit(blocks, 0)),
        name="qkv",
    )(x, gains, w_qkv, *tables)


def _attn_body(sink_ref, x_ref, qt_ref, kprev_ref, k_ref, knext_ref, vprev_ref, v_ref, vnext_ref, wo_ref, g_ref,
               o_ref, kx_ref, vw_ref, oh_ref, ta_ref, tb_ref, *, layer):
    i = pl.program_id(1)
    tq = x_ref.shape[1]
    blocks_per_tile = tq // BLOCK
    n_chains = blocks_per_tile * N_KV_HEADS
    n_blocks = pl.num_programs(1) * blocks_per_tile
    kx_ref[:, 0:BLOCK] = kprev_ref[0]
    kx_ref[:, BLOCK:BLOCK + tq] = k_ref[0]
    kx_ref[:, BLOCK + tq:] = knext_ref[0]
    vx = jnp.concatenate([vprev_ref[0], v_ref[0], vnext_ref[0]], axis=1)
    for n in range(blocks_per_tile):
        vw_ref[n] = vx[:, n * BLOCK:(n + 3) * BLOCK]

    log2e = math.log2(math.e)
    scale = HEAD_DIM ** -0.5 * log2e
    key = lax.broadcasted_iota(jnp.int32, (BLOCK, BLOCK), 0)
    qry = lax.broadcasted_iota(jnp.int32, (BLOCK, BLOCK), 1)

    def scores(c, t_ref):
        n, kv = c // N_KV_HEADS, c % N_KV_HEADS
        kb = kx_ref[kv, pl.ds(pl.multiple_of(n * BLOCK, BLOCK), 3 * BLOCK), 0:HEAD_DIM]
        t_ref[...] = jnp.dot(kb, qt_ref[n, kv], preferred_element_type=F32) * scale

    def finish(c, t_ref):
        n, kv = c // N_KV_HEADS, c % N_KV_HEADS
        gb = i * blocks_per_tile + n
        off_prev = jnp.where(gb == 0, BLOCK, BLOCK - WINDOW)
        off_next = jnp.where(gb == n_blocks - 1, BLOCK, BLOCK - WINDOW)
        mask_prev = jnp.concatenate([key >= qry + off_prev] * GROUP, axis=1)
        mask_next = jnp.concatenate([key <= qry - off_next] * GROUP, axis=1)
        t_prev = jnp.where(mask_prev, t_ref[0:BLOCK], NEG_INF)
        t_own = t_ref[BLOCK:2 * BLOCK]
        t_next = jnp.where(mask_next, t_ref[2 * BLOCK:], NEG_INF)
        sink = jnp.concatenate(
            [jnp.full((1, BLOCK), sink_ref[layer, kv * GROUP + hg] * log2e, F32) for hg in range(GROUP)], axis=1)
        m = jnp.maximum(jnp.maximum(t_prev, t_own), t_next)
        m = jnp.maximum(jnp.max(m, axis=0, keepdims=True), sink)
        e_prev, e_own, e_next = jnp.exp2(t_prev - m), jnp.exp2(t_own - m), jnp.exp2(t_next - m)
        denom = jnp.sum(e_prev + e_own + e_next, axis=0, keepdims=True) + jnp.exp2(sink - m)
        p = jnp.concatenate([e_prev, e_own, e_next], axis=0).astype(BF16)
        vb = vw_ref[n, pl.ds(pl.multiple_of(kv * HEAD_DIM, HEAD_DIM), HEAD_DIM), :]
        ot = jnp.dot(vb, p, preferred_element_type=F32) * (1.0 / denom)
        r0 = pl.multiple_of(n * BLOCK, BLOCK)
        for pair in range(GROUP // 2):
            two = jnp.concatenate([ot[:, (2 * pair) * BLOCK:(2 * pair + 1) * BLOCK],
                                   ot[:, (2 * pair + 1) * BLOCK:(2 * pair + 2) * BLOCK]], axis=0)
            oh_ref[kv * (GROUP // 2) + pair, pl.ds(r0, BLOCK), :] = two.T.astype(BF16)

    t_bufs = (ta_ref, tb_ref)
    scores(0, t_bufs[0])

    def step(j, carry):
        first = ATT_UNROLL * j
        for u in range(ATT_UNROLL):
            scores(jnp.minimum(first + u + 1, n_chains - 1), t_bufs[(u + 1) % 2])
            finish(first + u, t_bufs[u % 2])
        return carry

    lax.fori_loop(0, n_chains // ATT_UNROLL, step, 0)
    oh = jnp.concatenate([oh_ref[pp] for pp in range(N_HEADS // 2)], axis=1)
    att = jnp.dot(oh, wo_ref[...], preferred_element_type=F32)
    o_ref[0] = x_ref[0] + _rms(att, g_ref[...])


def _attn(x, qt, kp, vt, gains, k_gain, w_o, sink, j):
    b, s, _ = x.shape
    tq = ATT_TQ
    blocks_per_tile = tq // BLOCK
    tiles_per_seq = s // tq
    n_blocks = s // BLOCK
    main = lambda bi, i: (bi, i, 0)
    prev_blk = lambda i: jnp.maximum(i * blocks_per_tile - 1, 0)
    next_blk = lambda i: jnp.minimum((i + 1) * blocks_per_tile, n_blocks - 1)
    kx_shape = (N_KV_HEADS, tq + 2 * BLOCK, LANES)
    vw_shape = (blocks_per_tile, KV_DIM, 3 * BLOCK)
    oh_shape = (N_HEADS // 2, tq, LANES)
    t_shape = (3 * BLOCK, GROUP * BLOCK)
    blocks = (2 * _nbytes((tq, D_MODEL), F32) + _nbytes((tq, Q_DIM), BF16) + _nbytes(kx_shape, BF16)
              + _nbytes((KV_DIM, tq + 2 * BLOCK), BF16) + _nbytes((Q_DIM, D_MODEL), BF16) + _nbytes((1, D_MODEL), F32))
    scratch = (_nbytes(kx_shape, BF16) + _nbytes(vw_shape, BF16) + _nbytes(oh_shape, BF16)
               + 2 * _nbytes(t_shape, F32))
    return pl.pallas_call(
        functools.partial(_attn_body, layer=j),
        out_shape=jax.ShapeDtypeStruct((b, s, D_MODEL), F32),
        grid=(b, tiles_per_seq),
        in_specs=[
            pl.BlockSpec(memory_space=pltpu.SMEM),
            pl.BlockSpec((1, tq, D_MODEL), main),
            pl.BlockSpec((blocks_per_tile, N_KV_HEADS, HEAD_DIM, GROUP * BLOCK),
                         lambda bi, i: (bi * tiles_per_seq + i, 0, 0, 0)),
            pl.BlockSpec((1, N_KV_HEADS, BLOCK, LANES), lambda bi, i: (bi, 0, prev_blk(i), 0)),
            pl.BlockSpec((1, N_KV_HEADS, tq, LANES), lambda bi, i: (bi, 0, i, 0)),
            pl.BlockSpec((1, N_KV_HEADS, BLOCK, LANES), lambda bi, i: (bi, 0, next_blk(i), 0)),
            pl.BlockSpec((1, KV_DIM, BLOCK), lambda bi, i: (bi, 0, prev_blk(i))),
            pl.BlockSpec((1, KV_DIM, tq), lambda bi, i: (bi, 0, i)),
            pl.BlockSpec((1, KV_DIM, BLOCK), lambda bi, i: (bi, 0, next_blk(i))),
            _mat_spec(j, Q_DIM, D_MODEL),
            _row_spec(k_gain),
        ],
        out_specs=pl.BlockSpec((1, tq, D_MODEL), main),
        scratch_shapes=[pltpu.VMEM(kx_shape, BF16), pltpu.VMEM(vw_shape, BF16), pltpu.VMEM(oh_shape, BF16),
                        pltpu.VMEM(t_shape, F32), pltpu.VMEM(t_shape, F32)],
        compiler_params=pltpu.CompilerParams(
            dimension_semantics=("parallel", "arbitrary"),
            vmem_limit_bytes=_vmem_limit(blocks, scratch)),
        name="attn",
    )(sink, x, qt, kp, kp, kp, vt, vt, vt, w_o, gains)


def _trunk(x, p):
    b, s, _ = x.shape
    t = b * s
    gains = p["gains"]
    x = x.reshape(t, D_MODEL)

    def ffn(x, layer, which):
        k_pre = layer * N_SUBLAYER_NORMS + 4 * which
        return _ffn(x, gains, k_pre, p["wg"], p["wu"], p["wd"], layer, which)

    x = ffn(x, 0, 0)
    h = _glu(x, gains, 2, p["w_pw1"], p["b_pw1"], 0)
    x = _convmix(x.reshape(b, s, D_MODEL), h.reshape(b, s, D_MODEL), gains, 3, p["w_dw"], p["b_dw"], p["ln_g"],
                 p["ln_b"], p["w_pw2"], p["b_pw2"], 0).reshape(t, D_MODEL)
    x = ffn(x, 0, 1)

    x = ffn(x, 1, 0)
    qt, kp, vt = _qkv(x, gains, N_SUBLAYER_NORMS + 2, p["w_qkv"], 0, _rope_tables(s), b, s)
    x = _attn(x.reshape(b, s, D_MODEL), qt, kp, vt, gains, N_SUBLAYER_NORMS + 3, p["w_o"], p["sink"],
              0).reshape(t, D_MODEL)
    x = ffn(x, 1, 1)
    return x.reshape(b, s, D_MODEL)


def kernel(x_prompt, x_sample, norm_g, ffn_w_gate, ffn_w_up, ffn_w_down, conv_w_pw1, conv_b_pw1, conv_w_dw,
           conv_b_dw, conv_ln_g, conv_ln_b, conv_w_pw2, conv_b_pw2, attn_w_qkv, attn_w_o, attn_sink):
    rows = lambda a: a.reshape(-1, 1, a.shape[-1])
    p = {
        "gains": rows(norm_g),
        "wg": ffn_w_gate.astype(BF16), "wu": ffn_w_up.astype(BF16), "wd": ffn_w_down.astype(BF16),
        "w_pw1": conv_w_pw1.astype(BF16), "b_pw1": rows(conv_b_pw1),
        "w_dw": conv_w_dw, "b_dw": rows(conv_b_dw), "ln_g": rows(conv_ln_g), "ln_b": rows(conv_ln_b),
        "w_pw2": conv_w_pw2.astype(BF16), "b_pw2": rows(conv_b_pw2),
        "w_qkv": attn_w_qkv.astype(BF16), "w_o": attn_w_o.astype(BF16), "sink": attn_sink,
    }
    return _trunk(x_prompt, p), _trunk(x_sample, p)
```

```python
import functools
import math

import jax
import jax.numpy as jnp
from jax import lax
from jax.experimental import pallas as pl
from jax.experimental.pallas import tpu as pltpu

F32 = jnp.float32
BF16 = jnp.bfloat16

D_MODEL = 1024
D_FF = 4096
CONV_KERNEL = 31
CONV_HALF = CONV_KERNEL // 2
HEAD_DIM = 64
N_HEADS = 16
N_KV_HEADS = 4
GROUP = N_HEADS // N_KV_HEADS
Q_DIM = N_HEADS * HEAD_DIM
KV_DIM = N_KV_HEADS * HEAD_DIM
QKV_DIM = Q_DIM + 2 * KV_DIM
WINDOW = 128
BLOCK = 128
ROPE_THETA = 10000.0
EPS = 1e-6
NEG_INF = -1e30
N_SUBLAYER_NORMS = 6

LANES = 128
SUBLANES = 8
HALO_ROWS = 16
K_PAD_DIM = N_KV_HEADS * LANES

FFN_TM = 512
FFN_TF = 512
GLU_TM = 512
GLU_COLS = 256
CONV_TM = 512
CONV_ROWS = 128
QKV_TM = 512
ATT_TQ = 1024
ATT_UNROLL = 32

VMEM_TEMP_BYTES = 16 * 1024 * 1024


def _vmem_limit(pipelined_bytes, scratch_bytes):
    return int(2 * pipelined_bytes + scratch_bytes + VMEM_TEMP_BYTES)


def _nbytes(shape, dtype):
    return math.prod(shape) * jnp.dtype(dtype).itemsize


def _rms(x, g):
    ms = jnp.mean(x * x, axis=-1, keepdims=True)
    return x * lax.rsqrt(ms + EPS) * g


def _sigmoid(x):
    return 1.0 / (1.0 + jnp.exp(-x))


def _row_spec(k, width=D_MODEL):
    return pl.BlockSpec((None, 1, width), lambda *_: (k, 0, 0))


def _mat_spec(k, rows, cols):
    return pl.BlockSpec((None, rows, cols), lambda *_: (k, 0, 0))


def _ffn_body(x_ref, gpre_ref, gpost_ref, wg_ref, wu_ref, wd_ref, o_ref):
    x = x_ref[...]
    xn = _rms(x, gpre_ref[...]).astype(BF16)
    acc = None
    for c in range(D_FF // FFN_TF):
        cols = slice(c * FFN_TF, (c + 1) * FFN_TF)
        gate = jnp.dot(xn, wg_ref[:, cols], preferred_element_type=F32)
        up = jnp.dot(xn, wu_ref[:, cols], preferred_element_type=F32)
        h = (gate * _sigmoid(gate) * up).astype(BF16)
        part = jnp.dot(h, wd_ref[cols, :], preferred_element_type=F32)
        acc = part if acc is None else acc + part
    o_ref[...] = x + _rms(acc, 0.5 * gpost_ref[...])


def _ffn(x, gains, k_pre, wg, wu, wd, layer, which):
    t = x.shape[0]
    tm = FFN_TM
    resident = pl.Buffered(1)
    weights = 3 * _nbytes((D_MODEL, D_FF), BF16)
    blocks = 2 * _nbytes((tm, D_MODEL), F32) + 2 * _nbytes((1, D_MODEL), F32)
    return pl.pallas_call(
        _ffn_body,
        out_shape=jax.ShapeDtypeStruct((t, D_MODEL), F32),
        grid=(t // tm,),
        in_specs=[
            pl.BlockSpec((tm, D_MODEL), lambda i: (i, 0)),
            _row_spec(k_pre),
            _row_spec(k_pre + 1),
            pl.BlockSpec((None, None, D_MODEL, D_FF), lambda i: (layer, which, 0, 0), pipeline_mode=resident),
            pl.BlockSpec((None, None, D_MODEL, D_FF), lambda i: (layer, which, 0, 0), pipeline_mode=resident),
            pl.BlockSpec((None, None, D_FF, D_MODEL), lambda i: (layer, which, 0, 0), pipeline_mode=resident),
        ],
        out_specs=pl.BlockSpec((tm, D_MODEL), lambda i: (i, 0)),
        compiler_params=pltpu.CompilerParams(
            dimension_semantics=("parallel",),
            vmem_limit_bytes=_vmem_limit(blocks, weights)),
        name="ffn",
    )(x, gains, gains, wg, wu, wd)


def _glu_body(x_ref, g_ref, w_ref, b_ref, o_ref):
    hn = _rms(x_ref[...], g_ref[...]).astype(BF16)
    for j in range(D_MODEL // GLU_COLS):
        val = slice(j * GLU_COLS, (j + 1) * GLU_COLS)
        gate = slice(D_MODEL + j * GLU_COLS, D_MODEL + (j + 1) * GLU_COLS)
        a = jnp.dot(hn, w_ref[:, val], preferred_element_type=F32) + b_ref[:, val]
        g = jnp.dot(hn, w_ref[:, gate], preferred_element_type=F32) + b_ref[:, gate]
        o_ref[:, val] = a * _sigmoid(g)


def _glu(x, gains, k_gain, w_pw1, b_pw1, j):
    t = x.shape[0]
    tm = GLU_TM
    blocks = (2 * _nbytes((tm, D_MODEL), F32) + _nbytes((D_MODEL, 2 * D_MODEL), BF16)
              + 3 * _nbytes((1, D_MODEL), F32))
    return pl.pallas_call(
        _glu_body,
        out_shape=jax.ShapeDtypeStruct((t, D_MODEL), F32),
        grid=(t // tm,),
        in_specs=[
            pl.BlockSpec((tm, D_MODEL), lambda i: (i, 0)),
            _row_spec(k_gain),
            _mat_spec(j, D_MODEL, 2 * D_MODEL),
            _row_spec(j, 2 * D_MODEL),
        ],
        out_specs=pl.BlockSpec((tm, D_MODEL), lambda i: (i, 0)),
        compiler_params=pltpu.CompilerParams(
            dimension_semantics=("parallel",),
            vmem_limit_bytes=_vmem_limit(blocks, 0)),
        name="glu",
    )(x, gains, w_pw1, b_pw1)


def _convmix_body(x_ref, h_ref, hprev_ref, hnext_ref, wdw_ref, bdw_ref, lng_ref, lnb_ref,
                  w2_ref, b2_ref, g_ref, o_ref, hp_ref, c_ref):
    i = pl.program_id(1)
    tm = h_ref.shape[1]
    pieces = ((0, jnp.where(i > 0, hprev_ref[0], 0.0)),
              (HALO_ROWS, h_ref[0]),
              (HALO_ROWS + tm, jnp.where(i < pl.num_programs(1) - 1, hnext_ref[0], 0.0)))
    for row0, piece in pieces:
        for c in range(D_MODEL // LANES):
            hp_ref[c // 2, pl.ds(2 * row0 + c % 2, piece.shape[0], stride=2), :] = piece[:, c * LANES:(c + 1) * LANES]

    base = HALO_ROWS - CONV_HALF
    n_tiles = -(-(base + CONV_KERNEL) // SUBLANES)
    win = CONV_ROWS + (n_tiles - 1) * SUBLANES
    for c in range(D_MODEL // LANES):
        cols = slice(c * LANES, (c + 1) * LANES)

        def rows(r, carry, c=c, cols=cols):
            r0 = pl.multiple_of(r * CONV_ROWS, CONV_ROWS)
            acc = None
            for shift in range(SUBLANES):
                vs = hp_ref[c // 2, pl.ds(2 * (r0 + shift) + c % 2, win, stride=2), :]
                for tile in range(n_tiles):
                    k = tile * SUBLANES + shift - base
                    if 0 <= k < CONV_KERNEL:
                        term = vs[tile * SUBLANES:tile * SUBLANES + CONV_ROWS] * wdw_ref[k:k + 1, cols]
                        acc = term if acc is None else acc + term
            c_ref[pl.ds(r0, CONV_ROWS), cols] = acc
            return carry

        lax.fori_loop(0, tm // CONV_ROWS, rows, 0)

    y = c_ref[...] + bdw_ref[...]
    mu = jnp.mean(y, axis=-1, keepdims=True)
    yc = y - mu
    var = jnp.mean(yc * yc, axis=-1, keepdims=True)
    y = yc * lax.rsqrt(var + EPS) * lng_ref[...] + lnb_ref[...]
    y = (y * _sigmoid(y)).astype(BF16)
    h = jnp.dot(y, w2_ref[...], preferred_element_type=F32) + b2_ref[...]
    o_ref[0] = x_ref[0] + _rms(h, g_ref[...])


def _convmix(x, h, gains, k_gain, w_dw, b_dw, ln_g, ln_b, w_pw2, b_pw2, j):
    b, s, _ = x.shape
    tm = CONV_TM
    halo_per_tile = tm // HALO_ROWS
    n_halo = s // HALO_ROWS
    blocks = (3 * _nbytes((tm, D_MODEL), F32) + 2 * _nbytes((HALO_ROWS, D_MODEL), F32)
              + _nbytes((CONV_KERNEL, D_MODEL), F32) + _nbytes((D_MODEL, D_MODEL), BF16)
              + 6 * _nbytes((1, D_MODEL), F32))
    scratch = _nbytes((tm + 2 * HALO_ROWS, D_MODEL), F32) + _nbytes((tm, D_MODEL), F32)
    return pl.pallas_call(
        _convmix_body,
        out_shape=jax.ShapeDtypeStruct((b, s, D_MODEL), F32),
        grid=(b, s // tm),
        in_specs=[
            pl.BlockSpec((1, tm, D_MODEL), lambda bi, i: (bi, i, 0)),
            pl.BlockSpec((1, tm, D_MODEL), lambda bi, i: (bi, i, 0)),
            pl.BlockSpec((1, HALO_ROWS, D_MODEL),
                         lambda bi, i: (bi, jnp.maximum(i * halo_per_tile - 1, 0), 0)),
            pl.BlockSpec((1, HALO_ROWS, D_MODEL),
                         lambda bi, i: (bi, jnp.minimum((i + 1) * halo_per_tile, n_halo - 1), 0)),
            _mat_spec(j, CONV_KERNEL, D_MODEL),
            _row_spec(j), _row_spec(j), _row_spec(j),
            _mat_spec(j, D_MODEL, D_MODEL),
            _row_spec(j),
            _row_spec(k_gain),
        ],
        out_specs=pl.BlockSpec((1, tm, D_MODEL), lambda bi, i: (bi, i, 0)),
        scratch_shapes=[pltpu.VMEM((D_MODEL // (2 * LANES), 2 * (tm + 2 * HALO_ROWS), LANES), F32),
                        pltpu.VMEM((tm, D_MODEL), F32)],
        compiler_params=pltpu.CompilerParams(
            dimension_semantics=("parallel", "arbitrary"),
            vmem_limit_bytes=_vmem_limit(blocks, scratch)),
        name="convmix",
    )(x, h, h, h, w_dw, b_dw, ln_g, ln_b, w_pw2, b_pw2, gains)


def _rope_tables(s):
    half = HEAD_DIM // 2
    inv = ROPE_THETA ** (-jnp.arange(half, dtype=F32) / half)
    ang = jnp.arange(s).astype(F32)[:, None] * inv[None, :]
    cos, sin, zero = jnp.cos(ang), jnp.sin(ang), jnp.zeros_like(ang)
    reps = LANES // HEAD_DIM
    cos_row = jnp.tile(jnp.concatenate([cos, cos], axis=-1), (1, reps))
    sin_lo = jnp.tile(jnp.concatenate([-sin, zero], axis=-1), (1, reps))
    sin_hi = jnp.tile(jnp.concatenate([zero, sin], axis=-1), (1, reps))
    return cos_row, sin_lo, sin_hi, cos.T, sin.T


def _qkv_body(x_ref, g_ref, w_ref, cos_ref, slo_ref, shi_ref, cost_ref, sint_ref, qt_ref, kp_ref, vt_ref):
    tm = x_ref.shape[0]
    hn = _rms(x_ref[...], g_ref[...]).astype(BF16)
    kv = jnp.dot(hn, w_ref[:, Q_DIM:], preferred_element_type=F32)
    q = jnp.dot(hn, w_ref[:, :Q_DIM], preferred_element_type=F32)
    half = HEAD_DIM // 2
    heads_per_group = LANES // HEAD_DIM

    for c in range(Q_DIM // LANES):
        qc = q[:, c * LANES:(c + 1) * LANES]
        for n in range(tm // BLOCK):
            qct = qc[n * BLOCK:(n + 1) * BLOCK].T
            cos = cost_ref[:, n * BLOCK:(n + 1) * BLOCK]
            sin = sint_ref[:, n * BLOCK:(n + 1) * BLOCK]
            for hl in range(heads_per_group):
                head = c * heads_per_group + hl
                kvh, hg = head // GROUP, head % GROUP
                x1 = qct[hl * HEAD_DIM:hl * HEAD_DIM + half]
                x2 = qct[hl * HEAD_DIM + half:(hl + 1) * HEAD_DIM]
                roped = jnp.concatenate([x1 * cos - x2 * sin, x2 * cos + x1 * sin], axis=0)
                qt_ref[n, kvh, :, hg * BLOCK:(hg + 1) * BLOCK] = roped.astype(BF16)

    cos, slo, shi = cos_ref[...], slo_ref[...], shi_ref[...]
    for c in range(KV_DIM // LANES):
        kc = kv[:, c * LANES:(c + 1) * LANES]
        kc = kc * cos + pltpu.roll(kc, LANES - half, 1) * slo + pltpu.roll(kc, half, 1) * shi
        kp_ref[2 * c] = kc.astype(BF16)
        kp_ref[2 * c + 1] = pltpu.roll(kc, HEAD_DIM, 1).astype(BF16)
        vc = kv[:, KV_DIM + c * LANES:KV_DIM + (c + 1) * LANES]
        for n in range(tm // BLOCK):
            vt_ref[c * LANES:(c + 1) * LANES, n * BLOCK:(n + 1) * BLOCK] = (
                vc[n * BLOCK:(n + 1) * BLOCK].T.astype(BF16))


def _qkv(x, gains, k_gain, w_qkv, j, tables, b, s):
    t = x.shape[0]
    tm = QKV_TM
    tiles_per_seq = s // tm
    half = HEAD_DIM // 2
    blocks = (_nbytes((tm, D_MODEL), F32) + _nbytes((D_MODEL, QKV_DIM), BF16) + 3 * _nbytes((tm, LANES), F32)
              + 2 * _nbytes((half, tm), F32) + _nbytes((tm, Q_DIM + K_PAD_DIM + KV_DIM), BF16)
              + _nbytes((1, D_MODEL), F32))
    tab_row = pl.BlockSpec((tm, LANES), lambda i: (i % tiles_per_seq, 0))
    tab_col = pl.BlockSpec((half, tm), lambda i: (0, i % tiles_per_seq))
    return pl.pallas_call(
        _qkv_body,
        out_shape=(jax.ShapeDtypeStruct((t // BLOCK, N_KV_HEADS, HEAD_DIM, GROUP * BLOCK), BF16),
                   jax.ShapeDtypeStruct((b, N_KV_HEADS, s, LANES), BF16),
                   jax.ShapeDtypeStruct((b, KV_DIM, s), BF16)),
        grid=(t // tm,),
        in_specs=[
            pl.BlockSpec((tm, D_MODEL), lambda i: (i, 0)),
            _row_spec(k_gain),
            _mat_spec(j, D_MODEL, QKV_DIM),
            tab_row, tab_row, tab_row, tab_col, tab_col,
        ],
        out_specs=(pl.BlockSpec((tm // BLOCK, N_KV_HEADS, HEAD_DIM, GROUP * BLOCK), lambda i: (i, 0, 0, 0)),
                   pl.BlockSpec((None, N_KV_HEADS, tm, LANES),
                                lambda i: (i // tiles_per_seq, 0, i % tiles_per_seq, 0)),
                   pl.BlockSpec((None, KV_DIM, tm), lambda i: (i // tiles_per_seq, 0, i % tiles_per_seq))),
        compiler_params=pltpu.CompilerParams(
            dimension_semantics=("parallel",),
            vmem_limit_bytes=_vmem_limit(blocks, 0)),
        name="qkv",
    )(x, gains, w_qkv, *tables)


def _attn_body(sink_ref, x_ref, qt_ref, kprev_ref, k_ref, knext_ref, vprev_ref, v_ref, vnext_ref, wo_ref, g_ref,
               o_ref, kx_ref, vw_ref, oh_ref, ta_ref, tb_ref, *, layer):
    i = pl.program_id(1)
    tq = x_ref.shape[1]
    blocks_per_tile = tq // BLOCK
    n_chains = blocks_per_tile * N_KV_HEADS
    n_blocks = pl.num_programs(1) * blocks_per_tile
    kx_ref[:, 0:BLOCK] = kprev_ref[0]
    kx_ref[:, BLOCK:BLOCK + tq] = k_ref[0]
    kx_ref[:, BLOCK + tq:] = knext_ref[0]
    vx = jnp.concatenate([vprev_ref[0], v_ref[0], vnext_ref[0]], axis=1)
    for n in range(blocks_per_tile):
        vw_ref[n] = vx[:, n * BLOCK:(n + 3) * BLOCK]

    log2e = math.log2(math.e)
    scale = HEAD_DIM ** -0.5 * log2e
    key = lax.broadcasted_iota(jnp.int32, (BLOCK, BLOCK), 0)
    qry = lax.broadcasted_iota(jnp.int32, (BLOCK, BLOCK), 1)

    def scores(c, t_ref):
        n, kv = c // N_KV_HEADS, c % N_KV_HEADS
        kb = kx_ref[kv, pl.ds(pl.multiple_of(n * BLOCK, BLOCK), 3 * BLOCK), 0:HEAD_DIM]
        t_ref[...] = jnp.dot(kb, qt_ref[n, kv], preferred_element_type=F32) * scale

    def finish(c, t_ref):
        n, kv = c // N_KV_HEADS, c % N_KV_HEADS
        gb = i * blocks_per_tile + n
        off_prev = jnp.where(gb == 0, BLOCK, BLOCK - WINDOW)
        off_next = jnp.where(gb == n_blocks - 1, BLOCK, BLOCK - WINDOW)
        mask_prev = jnp.concatenate([key >= qry + off_prev] * GROUP, axis=1)
        mask_next = jnp.concatenate([key <= qry - off_next] * GROUP, axis=1)
        t_prev = jnp.where(mask_prev, t_ref[0:BLOCK], NEG_INF)
        t_own = t_ref[BLOCK:2 * BLOCK]
        t_next = jnp.where(mask_next, t_ref[2 * BLOCK:], NEG_INF)
        sink = jnp.concatenate(
            [jnp.full((1, BLOCK), sink_ref[layer, kv * GROUP + hg] * log2e, F32) for hg in range(GROUP)], axis=1)
        m = jnp.maximum(jnp.maximum(t_prev, t_own), t_next)
        m = jnp.maximum(jnp.max(m, axis=0, keepdims=True), sink)
        e_prev, e_own, e_next = jnp.exp2(t_prev - m), jnp.exp2(t_own - m), jnp.exp2(t_next - m)
        denom = jnp.sum(e_prev + e_own + e_next, axis=0, keepdims=True) + jnp.exp2(sink - m)
        p = jnp.concatenate([e_prev, e_own, e_next], axis=0).astype(BF16)
        vb = vw_ref[n, pl.ds(pl.multiple_of(kv * HEAD_DIM, HEAD_DIM), HEAD_DIM), :]
        ot = jnp.dot(vb, p, preferred_element_type=F32) * (1.0 / denom)
        r0 = pl.multiple_of(n * BLOCK, BLOCK)
        for pair in range(GROUP // 2):
            two = jnp.concatenate([ot[:, (2 * pair) * BLOCK:(2 * pair + 1) * BLOCK],
                                   ot[:, (2 * pair + 1) * BLOCK:(2 * pair + 2) * BLOCK]], axis=0)
            oh_ref[kv * (GROUP // 2) + pair, pl.ds(r0, BLOCK), :] = two.T.astype(BF16)

    t_bufs = (ta_ref, tb_ref)
    scores(0, t_bufs[0])

    def step(j, carry):
        first = ATT_UNROLL * j
        for u in range(ATT_UNROLL):
            scores(jnp.minimum(first + u + 1, n_chains - 1), t_bufs[(u + 1) % 2])
            finish(first + u, t_bufs[u % 2])
        return carry

    lax.fori_loop(0, n_chains // ATT_UNROLL, step, 0)
    oh = jnp.concatenate([oh_ref[pp] for pp in range(N_HEADS // 2)], axis=1)
    att = jnp.dot(oh, wo_ref[...], preferred_element_type=F32)
    o_ref[0] = x_ref[0] + _rms(att, g_ref[...])


def _attn(x, qt, kp, vt, gains, k_gain, w_o, sink, j):
    b, s, _ = x.shape
    tq = ATT_TQ
    blocks_per_tile = tq // BLOCK
    tiles_per_seq = s // tq
    n_blocks = s // BLOCK
    main = lambda bi, i: (bi, i, 0)
    prev_blk = lambda i: jnp.maximum(i * blocks_per_tile - 1, 0)
    next_blk = lambda i: jnp.minimum((i + 1) * blocks_per_tile, n_blocks - 1)
    kx_shape = (N_KV_HEADS, tq + 2 * BLOCK, LANES)
    vw_shape = (blocks_per_tile, KV_DIM, 3 * BLOCK)
    oh_shape = (N_HEADS // 2, tq, LANES)
    t_shape = (3 * BLOCK, GROUP * BLOCK)
    blocks = (2 * _nbytes((tq, D_MODEL), F32) + _nbytes((tq, Q_DIM), BF16) + _nbytes(kx_shape, BF16)
              + _nbytes((KV_DIM, tq + 2 * BLOCK), BF16) + _nbytes((Q_DIM, D_MODEL), BF16) + _nbytes((1, D_MODEL), F32))
    scratch = (_nbytes(kx_shape, BF16) + _nbytes(vw_shape, BF16) + _nbytes(oh_shape, BF16)
               + 2 * _nbytes(t_shape, F32))
    return pl.pallas_call(
        functools.partial(_attn_body, layer=j),
        out_shape=jax.ShapeDtypeStruct((b, s, D_MODEL), F32),
        grid=(b, tiles_per_seq),
        in_specs=[
            pl.BlockSpec(memory_space=pltpu.SMEM),
            pl.BlockSpec((1, tq, D_MODEL), main),
            pl.BlockSpec((blocks_per_tile, N_KV_HEADS, HEAD_DIM, GROUP * BLOCK),
                         lambda bi, i: (bi * tiles_per_seq + i, 0, 0, 0)),
            pl.BlockSpec((1, N_KV_HEADS, BLOCK, LANES), lambda bi, i: (bi, 0, prev_blk(i), 0)),
            pl.BlockSpec((1, N_KV_HEADS, tq, LANES), lambda bi, i: (bi, 0, i, 0)),
            pl.BlockSpec((1, N_KV_HEADS, BLOCK, LANES), lambda bi, i: (bi, 0, next_blk(i), 0)),
            pl.BlockSpec((1, KV_DIM, BLOCK), lambda bi, i: (bi, 0, prev_blk(i))),
            pl.BlockSpec((1, KV_DIM, tq), lambda bi, i: (bi, 0, i)),
            pl.BlockSpec((1, KV_DIM, BLOCK), lambda bi, i: (bi, 0, next_blk(i))),
            _mat_spec(j, Q_DIM, D_MODEL),
            _row_spec(k_gain),
        ],
        out_specs=pl.BlockSpec((1, tq, D_MODEL), main),
        scratch_shapes=[pltpu.VMEM(kx_shape, BF16), pltpu.VMEM(vw_shape, BF16), pltpu.VMEM(oh_shape, BF16),
                        pltpu.VMEM(t_shape, F32), pltpu.VMEM(t_shape, F32)],
        compiler_params=pltpu.CompilerParams(
            dimension_semantics=("parallel", "arbitrary"),
            vmem_limit_bytes=_vmem_limit(blocks, scratch)),
        name="attn",
    )(sink, x, qt, kp, kp, kp, vt, vt, vt, w_o, gains)


def _trunk(x, p):
    b, s, _ = x.shape
    t = b * s
    gains = p["gains"]
    x = x.reshape(t, D_MODEL)

    def ffn(x, layer, which):
        k_pre = layer * N_SUBLAYER_NORMS + 4 * which
        return _ffn(x, gains, k_pre, p["wg"], p["wu"], p["wd"], layer, which)

    x = ffn(x, 0, 0)
    h = _glu(x, gains, 2, p["w_pw1"], p["b_pw1"], 0)
    x = _convmix(x.reshape(b, s, D_MODEL), h.reshape(b, s, D_MODEL), gains, 3, p["w_dw"], p["b_dw"], p["ln_g"],
                 p["ln_b"], p["w_pw2"], p["b_pw2"], 0).reshape(t, D_MODEL)
    x = ffn(x, 0, 1)

    x = ffn(x, 1, 0)
    qt, kp, vt = _qkv(x, gains, N_SUBLAYER_NORMS + 2, p["w_qkv"], 0, p["rope"], b, s)
    x = _attn(x.reshape(b, s, D_MODEL), qt, kp, vt, gains, N_SUBLAYER_NORMS + 3, p["w_o"], p["sink"],
              0).reshape(t, D_MODEL)
    x = ffn(x, 1, 1)
    return x.reshape(b, s, D_MODEL)


def kernel(x_prompt, x_sample, norm_g, ffn_w_gate, ffn_w_up, ffn_w_down, conv_w_pw1, conv_b_pw1, conv_w_dw,
           conv_b_dw, conv_ln_g, conv_ln_b, conv_w_pw2, conv_b_pw2, attn_w_qkv, attn_w_o, attn_sink):
    rows = lambda a: a.reshape(-1, 1, a.shape[-1])
    p = {
        "gains": rows(norm_g),
        "wg": ffn_w_gate.astype(BF16), "wu": ffn_w_up.astype(BF16), "wd": ffn_w_down.astype(BF16),
        "w_pw1": conv_w_pw1.astype(BF16), "b_pw1": rows(conv_b_pw1),
        "w_dw": conv_w_dw, "b_dw": rows(conv_b_dw), "ln_g": rows(conv_ln_g), "ln_b": rows(conv_ln_b),
        "w_pw2": conv_w_pw2.astype(BF16), "b_pw2": rows(conv_b_pw2),
        "w_qkv": attn_w_qkv.astype(BF16), "w_o": attn_w_o.astype(BF16), "sink": attn_sink,
        "rope": _rope_tables(max(x_prompt.shape[1], x_sample.shape[1])),
    }
    return _trunk(x_prompt, p), _trunk(x_sample, p)
```

```python
import functools
import math

import jax
import jax.numpy as jnp
from jax import lax
from jax.experimental import pallas as pl
from jax.experimental.pallas import tpu as pltpu

F32 = jnp.float32
BF16 = jnp.bfloat16

D_MODEL = 1024
D_FF = 4096
CONV_KERNEL = 31
CONV_HALF = CONV_KERNEL // 2
HEAD_DIM = 64
N_HEADS = 16
N_KV_HEADS = 4
GROUP = N_HEADS // N_KV_HEADS
Q_DIM = N_HEADS * HEAD_DIM
KV_DIM = N_KV_HEADS * HEAD_DIM
QKV_DIM = Q_DIM + 2 * KV_DIM
WINDOW = 128
BLOCK = 128
ROPE_THETA = 10000.0
EPS = 1e-6
NEG_INF = -1e30
N_SUBLAYER_NORMS = 6

LANES = 128
SUBLANES = 8
HALO_ROWS = 16
K_PAD_DIM = N_KV_HEADS * LANES

FFN_TM = 512
FFN_TF = 512
GLU_TM = 512
GLU_COLS = 256
CONV_TM = 512
CONV_ROWS = 128
CONV_UNROLL = 2
QKV_TM = 512
ATT_TQ = 1024
ATT_UNROLL = 32

VMEM_TEMP_BYTES = 16 * 1024 * 1024


def _vmem_limit(pipelined_bytes, scratch_bytes):
    return int(2 * pipelined_bytes + scratch_bytes + VMEM_TEMP_BYTES)


def _nbytes(shape, dtype):
    return math.prod(shape) * jnp.dtype(dtype).itemsize


def _rms(x, g):
    ms = jnp.mean(x * x, axis=-1, keepdims=True)
    return x * lax.rsqrt(ms + EPS) * g


def _sigmoid(x):
    return 1.0 / (1.0 + jnp.exp(-x))


def _row_spec(k, width=D_MODEL):
    return pl.BlockSpec((None, 1, width), lambda *_: (k, 0, 0))


def _mat_spec(k, rows, cols):
    return pl.BlockSpec((None, rows, cols), lambda *_: (k, 0, 0))


def _ffn_body(x_ref, gpre_ref, gpost_ref, wg_ref, wu_ref, wd_ref, o_ref):
    x = x_ref[...]
    xn = _rms(x, gpre_ref[...]).astype(BF16)
    acc = None
    for c in range(D_FF // FFN_TF):
        cols = slice(c * FFN_TF, (c + 1) * FFN_TF)
        gate = jnp.dot(xn, wg_ref[:, cols], preferred_element_type=F32)
        up = jnp.dot(xn, wu_ref[:, cols], preferred_element_type=F32)
        h = (gate * _sigmoid(gate) * up).astype(BF16)
        part = jnp.dot(h, wd_ref[cols, :], preferred_element_type=F32)
        acc = part if acc is None else acc + part
    o_ref[...] = x + _rms(acc, 0.5 * gpost_ref[...])


def _ffn(x, gains, k_pre, wg, wu, wd, layer, which):
    t = x.shape[0]
    tm = FFN_TM
    resident = pl.Buffered(1)
    weights = 3 * _nbytes((D_MODEL, D_FF), BF16)
    blocks = 2 * _nbytes((tm, D_MODEL), F32) + 2 * _nbytes((1, D_MODEL), F32)
    return pl.pallas_call(
        _ffn_body,
        out_shape=jax.ShapeDtypeStruct((t, D_MODEL), F32),
        grid=(t // tm,),
        in_specs=[
            pl.BlockSpec((tm, D_MODEL), lambda i: (i, 0)),
            _row_spec(k_pre),
            _row_spec(k_pre + 1),
            pl.BlockSpec((None, None, D_MODEL, D_FF), lambda i: (layer, which, 0, 0), pipeline_mode=resident),
            pl.BlockSpec((None, None, D_MODEL, D_FF), lambda i: (layer, which, 0, 0), pipeline_mode=resident),
            pl.BlockSpec((None, None, D_FF, D_MODEL), lambda i: (layer, which, 0, 0), pipeline_mode=resident),
        ],
        out_specs=pl.BlockSpec((tm, D_MODEL), lambda i: (i, 0)),
        compiler_params=pltpu.CompilerParams(
            dimension_semantics=("parallel",),
            vmem_limit_bytes=_vmem_limit(blocks, weights)),
        name="ffn",
    )(x, gains, gains, wg, wu, wd)


def _glu_body(x_ref, g_ref, w_ref, b_ref, o_ref):
    hn = _rms(x_ref[...], g_ref[...]).astype(BF16)
    for j in range(D_MODEL // GLU_COLS):
        val = slice(j * GLU_COLS, (j + 1) * GLU_COLS)
        gate = slice(D_MODEL + j * GLU_COLS, D_MODEL + (j + 1) * GLU_COLS)
        a = jnp.dot(hn, w_ref[:, val], preferred_element_type=F32) + b_ref[:, val]
        g = jnp.dot(hn, w_ref[:, gate], preferred_element_type=F32) + b_ref[:, gate]
        o_ref[:, val] = a * _sigmoid(g)


def _glu(x, gains, k_gain, w_pw1, b_pw1, j):
    t = x.shape[0]
    tm = GLU_TM
    blocks = (2 * _nbytes((tm, D_MODEL), F32) + _nbytes((D_MODEL, 2 * D_MODEL), BF16)
              + 3 * _nbytes((1, D_MODEL), F32))
    return pl.pallas_call(
        _glu_body,
        out_shape=jax.ShapeDtypeStruct((t, D_MODEL), F32),
        grid=(t // tm,),
        in_specs=[
            pl.BlockSpec((tm, D_MODEL), lambda i: (i, 0)),
            _row_spec(k_gain),
            _mat_spec(j, D_MODEL, 2 * D_MODEL),
            _row_spec(j, 2 * D_MODEL),
        ],
        out_specs=pl.BlockSpec((tm, D_MODEL), lambda i: (i, 0)),
        compiler_params=pltpu.CompilerParams(
            dimension_semantics=("parallel",),
            vmem_limit_bytes=_vmem_limit(blocks, 0)),
        name="glu",
    )(x, gains, w_pw1, b_pw1)


def _convmix_body(x_ref, h_ref, hprev_ref, hnext_ref, wdw_ref, bdw_ref, lng_ref, lnb_ref,
                  w2_ref, b2_ref, g_ref, o_ref, hp_ref, c_ref):
    i = pl.program_id(1)
    tm = h_ref.shape[1]
    pieces = ((0, jnp.where(i > 0, hprev_ref[0], 0.0)),
              (HALO_ROWS, h_ref[0]),
              (HALO_ROWS + tm, jnp.where(i < pl.num_programs(1) - 1, hnext_ref[0], 0.0)))
    for row0, piece in pieces:
        for c in range(D_MODEL // LANES):
            hp_ref[c // 2, pl.ds(2 * row0 + c % 2, piece.shape[0], stride=2), :] = piece[:, c * LANES:(c + 1) * LANES]

    base = HALO_ROWS - CONV_HALF
    n_tiles = -(-(base + CONV_KERNEL) // SUBLANES)
    win = CONV_ROWS + (n_tiles - 1) * SUBLANES
    for c in range(D_MODEL // LANES):
        cols = slice(c * LANES, (c + 1) * LANES)

        def rows(r, carry, c=c, cols=cols):
            r0 = pl.multiple_of(r * CONV_ROWS, CONV_ROWS)
            acc = None
            for shift in range(SUBLANES):
                vs = hp_ref[c // 2, pl.ds(2 * (r0 + shift) + c % 2, win, stride=2), :]
                for tile in range(n_tiles):
                    k = tile * SUBLANES + shift - base
                    if 0 <= k < CONV_KERNEL:
                        term = vs[tile * SUBLANES:tile * SUBLANES + CONV_ROWS] * wdw_ref[k:k + 1, cols]
                        acc = term if acc is None else acc + term
            c_ref[pl.ds(r0, CONV_ROWS), cols] = acc
            return carry

        lax.fori_loop(0, tm // CONV_ROWS, rows, 0, unroll=CONV_UNROLL)

    y = c_ref[...] + bdw_ref[...]
    mu = jnp.mean(y, axis=-1, keepdims=True)
    yc = y - mu
    var = jnp.mean(yc * yc, axis=-1, keepdims=True)
    y = yc * lax.rsqrt(var + EPS) * lng_ref[...] + lnb_ref[...]
    y = (y * _sigmoid(y)).astype(BF16)
    h = jnp.dot(y, w2_ref[...], preferred_element_type=F32) + b2_ref[...]
    o_ref[0] = x_ref[0] + _rms(h, g_ref[...])


def _convmix(x, h, gains, k_gain, w_dw, b_dw, ln_g, ln_b, w_pw2, b_pw2, j):
    b, s, _ = x.shape
    tm = CONV_TM
    halo_per_tile = tm // HALO_ROWS
    n_halo = s // HALO_ROWS
    blocks = (3 * _nbytes((tm, D_MODEL), F32) + 2 * _nbytes((HALO_ROWS, D_MODEL), F32)
              + _nbytes((CONV_KERNEL, D_MODEL), F32) + _nbytes((D_MODEL, D_MODEL), BF16)
              + 6 * _nbytes((1, D_MODEL), F32))
    scratch = _nbytes((tm + 2 * HALO_ROWS, D_MODEL), F32) + _nbytes((tm, D_MODEL), F32)
    return pl.pallas_call(
        _convmix_body,
        out_shape=jax.ShapeDtypeStruct((b, s, D_MODEL), F32),
        grid=(b, s // tm),
        in_specs=[
            pl.BlockSpec((1, tm, D_MODEL), lambda bi, i: (bi, i, 0)),
            pl.BlockSpec((1, tm, D_MODEL), lambda bi, i: (bi, i, 0)),
            pl.BlockSpec((1, HALO_ROWS, D_MODEL),
                         lambda bi, i: (bi, jnp.maximum(i * halo_per_tile - 1, 0), 0)),
            pl.BlockSpec((1, HALO_ROWS, D_MODEL),
                         lambda bi, i: (bi, jnp.minimum((i + 1) * halo_per_tile, n_halo - 1), 0)),
            _mat_spec(j, CONV_KERNEL, D_MODEL),
            _row_spec(j), _row_spec(j), _row_spec(j),
            _mat_spec(j, D_MODEL, D_MODEL),
            _row_spec(j),
            _row_spec(k_gain),
        ],
        out_specs=pl.BlockSpec((1, tm, D_MODEL), lambda bi, i: (bi, i, 0)),
        scratch_shapes=[pltpu.VMEM((D_MODEL // (2 * LANES), 2 * (tm + 2 * HALO_ROWS), LANES), F32),
                        pltpu.VMEM((tm, D_MODEL), F32)],
        compiler_params=pltpu.CompilerParams(
            dimension_semantics=("parallel", "arbitrary"),
            vmem_limit_bytes=_vmem_limit(blocks, scratch)),
        name="convmix",
    )(x, h, h, h, w_dw, b_dw, ln_g, ln_b, w_pw2, b_pw2, gains)


def _rope_tables(s):
    half = HEAD_DIM // 2
    inv = ROPE_THETA ** (-jnp.arange(half, dtype=F32) / half)
    ang = jnp.arange(s).astype(F32)[:, None] * inv[None, :]
    cos, sin, zero = jnp.cos(ang), jnp.sin(ang), jnp.zeros_like(ang)
    reps = LANES // HEAD_DIM
    cos_row = jnp.tile(jnp.concatenate([cos, cos], axis=-1), (1, reps))
    sin_lo = jnp.tile(jnp.concatenate([-sin, zero], axis=-1), (1, reps))
    sin_hi = jnp.tile(jnp.concatenate([zero, sin], axis=-1), (1, reps))
    return cos_row, sin_lo, sin_hi, cos.T, sin.T


def _qkv_body(x_ref, g_ref, w_ref, cos_ref, slo_ref, shi_ref, cost_ref, sint_ref, qt_ref, kp_ref, vt_ref):
    tm = x_ref.shape[0]
    hn = _rms(x_ref[...], g_ref[...]).astype(BF16)
    kv = jnp.dot(hn, w_ref[:, Q_DIM:], preferred_element_type=F32)
    q = jnp.dot(hn, w_ref[:, :Q_DIM], preferred_element_type=F32)
    half = HEAD_DIM // 2
    heads_per_group = LANES // HEAD_DIM

    for c in range(Q_DIM // LANES):
        qc = q[:, c * LANES:(c + 1) * LANES]
        for n in range(tm // BLOCK):
            qct = qc[n * BLOCK:(n + 1) * BLOCK].T
            cos = cost_ref[:, n * BLOCK:(n + 1) * BLOCK]
            sin = sint_ref[:, n * BLOCK:(n + 1) * BLOCK]
            for hl in range(heads_per_group):
                head = c * heads_per_group + hl
                kvh, hg = head // GROUP, head % GROUP
                x1 = qct[hl * HEAD_DIM:hl * HEAD_DIM + half]
                x2 = qct[hl * HEAD_DIM + half:(hl + 1) * HEAD_DIM]
                roped = jnp.concatenate([x1 * cos - x2 * sin, x2 * cos + x1 * sin], axis=0)
                qt_ref[n, kvh, :, hg * BLOCK:(hg + 1) * BLOCK] = roped.astype(BF16)

    cos, slo, shi = cos_ref[...], slo_ref[...], shi_ref[...]
    for c in range(KV_DIM // LANES):
        kc = kv[:, c * LANES:(c + 1) * LANES]
        kc = kc * cos + pltpu.roll(kc, LANES - half, 1) * slo + pltpu.roll(kc, half, 1) * shi
        kp_ref[2 * c] = kc.astype(BF16)
        kp_ref[2 * c + 1] = pltpu.roll(kc, HEAD_DIM, 1).astype(BF16)
        vc = kv[:, KV_DIM + c * LANES:KV_DIM + (c + 1) * LANES]
        for n in range(tm // BLOCK):
            vt_ref[c * LANES:(c + 1) * LANES, n * BLOCK:(n + 1) * BLOCK] = (
                vc[n * BLOCK:(n + 1) * BLOCK].T.astype(BF16))


def _qkv(x, gains, k_gain, w_qkv, j, tables, b, s):
    t = x.shape[0]
    tm = QKV_TM
    tiles_per_seq = s // tm
    half = HEAD_DIM // 2
    blocks = (_nbytes((tm, D_MODEL), F32) + _nbytes((D_MODEL, QKV_DIM), BF16) + 3 * _nbytes((tm, LANES), F32)
              + 2 * _nbytes((half, tm), F32) + _nbytes((tm, Q_DIM + K_PAD_DIM + KV_DIM), BF16)
              + _nbytes((1, D_MODEL), F32))
    tab_row = pl.BlockSpec((tm, LANES), lambda i: (i % tiles_per_seq, 0))
    tab_col = pl.BlockSpec((half, tm), lambda i: (0, i % tiles_per_seq))
    return pl.pallas_call(
        _qkv_body,
        out_shape=(jax.ShapeDtypeStruct((t // BLOCK, N_KV_HEADS, HEAD_DIM, GROUP * BLOCK), BF16),
                   jax.ShapeDtypeStruct((b, N_KV_HEADS, s, LANES), BF16),
                   jax.ShapeDtypeStruct((b, KV_DIM, s), BF16)),
        grid=(t // tm,),
        in_specs=[
            pl.BlockSpec((tm, D_MODEL), lambda i: (i, 0)),
            _row_spec(k_gain),
            _mat_spec(j, D_MODEL, QKV_DIM),
            tab_row, tab_row, tab_row, tab_col, tab_col,
        ],
        out_specs=(pl.BlockSpec((tm // BLOCK, N_KV_HEADS, HEAD_DIM, GROUP * BLOCK), lambda i: (i, 0, 0, 0)),
                   pl.BlockSpec((None, N_KV_HEADS, tm, LANES),
                                lambda i: (i // tiles_per_seq, 0, i % tiles_per_seq, 0)),
                   pl.BlockSpec((None, KV_DIM, tm), lambda i: (i // tiles_per_seq, 0, i % tiles_per_seq))),
        compiler_params=pltpu.CompilerParams(
            dimension_semantics=("parallel",),
            vmem_limit_bytes=_vmem_limit(blocks, 0)),
        name="qkv",
    )(x, gains, w_qkv, *tables)


def _attn_body(sink_ref, x_ref, qt_ref, kprev_ref, k_ref, knext_ref, vprev_ref, v_ref, vnext_ref, wo_ref, g_ref,
               o_ref, kx_ref, vw_ref, oh_ref, ta_ref, tb_ref, *, layer):
    i = pl.program_id(1)
    tq = x_ref.shape[1]
    blocks_per_tile = tq // BLOCK
    n_chains = blocks_per_tile * N_KV_HEADS
    n_blocks = pl.num_programs(1) * blocks_per_tile
    kx_ref[:, 0:BLOCK] = kprev_ref[0]
    kx_ref[:, BLOCK:BLOCK + tq] = k_ref[0]
    kx_ref[:, BLOCK + tq:] = knext_ref[0]
    vx = jnp.concatenate([vprev_ref[0], v_ref[0], vnext_ref[0]], axis=1)
    for n in range(blocks_per_tile):
        vw_ref[n] = vx[:, n * BLOCK:(n + 3) * BLOCK]

    log2e = math.log2(math.e)
    scale = HEAD_DIM ** -0.5 * log2e
    key = lax.broadcasted_iota(jnp.int32, (BLOCK, BLOCK), 0)
    qry = lax.broadcasted_iota(jnp.int32, (BLOCK, BLOCK), 1)

    def scores(c, t_ref):
        n, kv = c // N_KV_HEADS, c % N_KV_HEADS
        kb = kx_ref[kv, pl.ds(pl.multiple_of(n * BLOCK, BLOCK), 3 * BLOCK), 0:HEAD_DIM]
        t_ref[...] = jnp.dot(kb, qt_ref[n, kv], preferred_element_type=F32) * scale

    def finish(c, t_ref):
        n, kv = c // N_KV_HEADS, c % N_KV_HEADS
        gb = i * blocks_per_tile + n
        off_prev = jnp.where(gb == 0, BLOCK, BLOCK - WINDOW)
        off_next = jnp.where(gb == n_blocks - 1, BLOCK, BLOCK - WINDOW)
        mask_prev = jnp.concatenate([key >= qry + off_prev] * GROUP, axis=1)
        mask_next = jnp.concatenate([key <= qry - off_next] * GROUP, axis=1)
        t_prev = jnp.where(mask_prev, t_ref[0:BLOCK], NEG_INF)
        t_own = t_ref[BLOCK:2 * BLOCK]
        t_next = jnp.where(mask_next, t_ref[2 * BLOCK:], NEG_INF)
        sink = jnp.concatenate(
            [jnp.full((1, BLOCK), sink_ref[layer, kv * GROUP + hg] * log2e, F32) for hg in range(GROUP)], axis=1)
        m = jnp.maximum(jnp.maximum(t_prev, t_own), t_next)
        m = jnp.maximum(jnp.max(m, axis=0, keepdims=True), sink)
        e_prev, e_own, e_next = jnp.exp2(t_prev - m), jnp.exp2(t_own - m), jnp.exp2(t_next - m)
        denom = jnp.sum(e_prev + e_own + e_next, axis=0, keepdims=True) + jnp.exp2(sink - m)
        p = jnp.concatenate([e_prev, e_own, e_next], axis=0).astype(BF16)
        vb = vw_ref[n, pl.ds(pl.multiple_of(kv * HEAD_DIM, HEAD_DIM), HEAD_DIM), :]
        ot = jnp.dot(vb, p, preferred_element_type=F32) * (1.0 / denom)
        r0 = pl.multiple_of(n * BLOCK, BLOCK)
        for pair in range(GROUP // 2):
            two = jnp.concatenate([ot[:, (2 * pair) * BLOCK:(2 * pair + 1) * BLOCK],
                                   ot[:, (2 * pair + 1) * BLOCK:(2 * pair + 2) * BLOCK]], axis=0)
            oh_ref[kv * (GROUP // 2) + pair, pl.ds(r0, BLOCK), :] = two.T.astype(BF16)

    t_bufs = (ta_ref, tb_ref)
    scores(0, t_bufs[0])

    def step(j, carry):
        first = ATT_UNROLL * j
        for u in range(ATT_UNROLL):
            scores(jnp.minimum(first + u + 1, n_chains - 1), t_bufs[(u + 1) % 2])
            finish(first + u, t_bufs[u % 2])
        return carry

    lax.fori_loop(0, n_chains // ATT_UNROLL, step, 0)
    oh = jnp.concatenate([oh_ref[pp] for pp in range(N_HEADS // 2)], axis=1)
    att = jnp.dot(oh, wo_ref[...], preferred_element_type=F32)
    o_ref[0] = x_ref[0] + _rms(att, g_ref[...])


def _attn(x, qt, kp, vt, gains, k_gain, w_o, sink, j):
    b, s, _ = x.shape
    tq = ATT_TQ
    blocks_per_tile = tq // BLOCK
    tiles_per_seq = s // tq
    n_blocks = s // BLOCK
    main = lambda bi, i: (bi, i, 0)
    prev_blk = lambda i: jnp.maximum(i * blocks_per_tile - 1, 0)
    next_blk = lambda i: jnp.minimum((i + 1) * blocks_per_tile, n_blocks - 1)
    kx_shape = (N_KV_HEADS, tq + 2 * BLOCK, LANES)
    vw_shape = (blocks_per_tile, KV_DIM, 3 * BLOCK)
    oh_shape = (N_HEADS // 2, tq, LANES)
    t_shape = (3 * BLOCK, GROUP * BLOCK)
    blocks = (2 * _nbytes((tq, D_MODEL), F32) + _nbytes((tq, Q_DIM), BF16) + _nbytes(kx_shape, BF16)
              + _nbytes((KV_DIM, tq + 2 * BLOCK), BF16) + _nbytes((Q_DIM, D_MODEL), BF16) + _nbytes((1, D_MODEL), F32))
    scratch = (_nbytes(kx_shape, BF16) + _nbytes(vw_shape, BF16) + _nbytes(oh_shape, BF16)
               + 2 * _nbytes(t_shape, F32))
    return pl.pallas_call(
        functools.partial(_attn_body, layer=j),
        out_shape=jax.ShapeDtypeStruct((b, s, D_MODEL), F32),
        grid=(b, tiles_per_seq),
        in_specs=[
            pl.BlockSpec(memory_space=pltpu.SMEM),
            pl.BlockSpec((1, tq, D_MODEL), main),
            pl.BlockSpec((blocks_per_tile, N_KV_HEADS, HEAD_DIM, GROUP * BLOCK),
                         lambda bi, i: (bi * tiles_per_seq + i, 0, 0, 0)),
            pl.BlockSpec((1, N_KV_HEADS, BLOCK, LANES), lambda bi, i: (bi, 0, prev_blk(i), 0)),
            pl.BlockSpec((1, N_KV_HEADS, tq, LANES), lambda bi, i: (bi, 0, i, 0)),
            pl.BlockSpec((1, N_KV_HEADS, BLOCK, LANES), lambda bi, i: (bi, 0, next_blk(i), 0)),
            pl.BlockSpec((1, KV_DIM, BLOCK), lambda bi, i: (bi, 0, prev_blk(i))),
            pl.BlockSpec((1, KV_DIM, tq), lambda bi, i: (bi, 0, i)),
            pl.BlockSpec((1, KV_DIM, BLOCK), lambda bi, i: (bi, 0, next_blk(i))),
            _mat_spec(j, Q_DIM, D_MODEL),
            _row_spec(k_gain),
        ],
        out_specs=pl.BlockSpec((1, tq, D_MODEL), main),
        scratch_shapes=[pltpu.VMEM(kx_shape, BF16), pltpu.VMEM(vw_shape, BF16), pltpu.VMEM(oh_shape, BF16),
                        pltpu.VMEM(t_shape, F32), pltpu.VMEM(t_shape, F32)],
        compiler_params=pltpu.CompilerParams(
            dimension_semantics=("parallel", "arbitrary"),
            vmem_limit_bytes=_vmem_limit(blocks, scratch)),
        name="attn",
    )(sink, x, qt, kp, kp, kp, vt, vt, vt, w_o, gains)


def _trunk(x, p):
    b, s, _ = x.shape
    t = b * s
    gains = p["gains"]
    x = x.reshape(t, D_MODEL)

    def ffn(x, layer, which):
        k_pre = layer * N_SUBLAYER_NORMS + 4 * which
        return _ffn(x, gains, k_pre, p["wg"], p["wu"], p["wd"], layer, which)

    x = ffn(x, 0, 0)
    h = _glu(x, gains, 2, p["w_pw1"], p["b_pw1"], 0)
    x = _convmix(x.reshape(b, s, D_MODEL), h.reshape(b, s, D_MODEL), gains, 3, p["w_dw"], p["b_dw"], p["ln_g"],
                 p["ln_b"], p["w_pw2"], p["b_pw2"], 0).reshape(t, D_MODEL)
    x = ffn(x, 0, 1)

    x = ffn(x, 1, 0)
    qt, kp, vt = _qkv(x, gains, N_SUBLAYER_NORMS + 2, p["w_qkv"], 0, p["rope"], b, s)
    x = _attn(x.reshape(b, s, D_MODEL), qt, kp, vt, gains, N_SUBLAYER_NORMS + 3, p["w_o"], p["sink"],
              0).reshape(t, D_MODEL)
    x = ffn(x, 1, 1)
    return x.reshape(b, s, D_MODEL)


def kernel(x_prompt, x_sample, norm_g, ffn_w_gate, ffn_w_up, ffn_w_down, conv_w_pw1, conv_b_pw1, conv_w_dw,
           conv_b_dw, conv_ln_g, conv_ln_b, conv_w_pw2, conv_b_pw2, attn_w_qkv, attn_w_o, attn_sink):
    rows = lambda a: a.reshape(-1, 1, a.shape[-1])
    p = {
        "gains": rows(norm_g),
        "wg": ffn_w_gate.astype(BF16), "wu": ffn_w_up.astype(BF16), "wd": ffn_w_down.astype(BF16),
        "w_pw1": conv_w_pw1.astype(BF16), "b_pw1": rows(conv_b_pw1),
        "w_dw": conv_w_dw, "b_dw": rows(conv_b_dw), "ln_g": rows(conv_ln_g), "ln_b": rows(conv_ln_b),
        "w_pw2": conv_w_pw2.astype(BF16), "b_pw2": rows(conv_b_pw2),
        "w_qkv": attn_w_qkv.astype(BF16), "w_o": attn_w_o.astype(BF16), "sink": attn_sink,
        "rope": _rope_tables(max(x_prompt.shape[1], x_sample.shape[1])),
    }
    return _trunk(x_prompt, p), _trunk(x_sample, p)
```

```python
import functools
import math

import jax
import jax.numpy as jnp
from jax import lax
from jax.experimental import pallas as pl
from jax.experimental.pallas import tpu as pltpu

F32 = jnp.float32
BF16 = jnp.bfloat16

D_MODEL = 1024
D_FF = 4096
CONV_KERNEL = 31
CONV_HALF = CONV_KERNEL // 2
HEAD_DIM = 64
N_HEADS = 16
N_KV_HEADS = 4
GROUP = N_HEADS // N_KV_HEADS
Q_DIM = N_HEADS * HEAD_DIM
KV_DIM = N_KV_HEADS * HEAD_DIM
QKV_DIM = Q_DIM + 2 * KV_DIM
WINDOW = 128
BLOCK = 128
ROPE_THETA = 10000.0
EPS = 1e-6
NEG_INF = -1e30
N_SUBLAYER_NORMS = 6

LANES = 128
SUBLANES = 8
HALO_ROWS = 16
K_PAD_DIM = N_KV_HEADS * LANES

FFN_TM = 512
FFN_TF = 512
GLU_COLS = 256
CONV_TM = 512
CONV_ROWS = 128
CONV_UNROLL = 2
ATT_TQ = 1024
ATT_UNROLL = 32

VMEM_TEMP_BYTES = 16 * 1024 * 1024


def _vmem_limit(pipelined_bytes, scratch_bytes):
    return int(2 * pipelined_bytes + scratch_bytes + VMEM_TEMP_BYTES)


def _nbytes(shape, dtype):
    return math.prod(shape) * jnp.dtype(dtype).itemsize


def _rms(x, g):
    ms = jnp.mean(x * x, axis=-1, keepdims=True)
    return x * lax.rsqrt(ms + EPS) * g


def _sigmoid(x):
    return 1.0 / (1.0 + jnp.exp(-x))


def _row_spec(k, width=D_MODEL):
    return pl.BlockSpec((None, 1, width), lambda *_: (k, 0, 0))


def _mat_spec(k, rows, cols):
    return pl.BlockSpec((None, rows, cols), lambda *_: (k, 0, 0))


def _ffn_body(x_ref, gpre_ref, gpost_ref, wg_ref, wu_ref, wd_ref, *rest, n_tail_in, tail_stage):
    tail_in, o_ref, tail_out = rest[:n_tail_in], rest[n_tail_in], rest[n_tail_in + 1:]
    x = x_ref[...]
    xn = _rms(x, gpre_ref[...]).astype(BF16)
    acc = None
    for c in range(D_FF // FFN_TF):
        cols = slice(c * FFN_TF, (c + 1) * FFN_TF)
        gate = jnp.dot(xn, wg_ref[:, cols], preferred_element_type=F32)
        up = jnp.dot(xn, wu_ref[:, cols], preferred_element_type=F32)
        h = (gate * _sigmoid(gate) * up).astype(BF16)
        part = jnp.dot(h, wd_ref[cols, :], preferred_element_type=F32)
        acc = part if acc is None else acc + part
    y = x + _rms(acc, 0.5 * gpost_ref[...])
    o_ref[...] = y
    if tail_stage is not None:
        tail_stage(y, *tail_in, *tail_out)


def _ffn(x, gains, k_pre, wg, wu, wd, layer, which, tail=None):
    t = x.shape[0]
    tm = FFN_TM
    resident = pl.Buffered(1)
    stage, operands, tail_in_specs, tail_shapes, tail_out_specs, tail_resident, tail_blocks, name = (
        tail if tail is not None else (None, (), [], (), (), 0, 0, "ffn"))
    weights = 3 * _nbytes((D_MODEL, D_FF), BF16) + tail_resident
    blocks = 2 * _nbytes((tm, D_MODEL), F32) + 2 * _nbytes((1, D_MODEL), F32) + tail_blocks
    out = pl.pallas_call(
        functools.partial(_ffn_body, n_tail_in=len(operands), tail_stage=stage),
        out_shape=(jax.ShapeDtypeStruct((t, D_MODEL), F32), *tail_shapes),
        grid=(t // tm,),
        in_specs=[
            pl.BlockSpec((tm, D_MODEL), lambda i: (i, 0)),
            _row_spec(k_pre),
            _row_spec(k_pre + 1),
            pl.BlockSpec((None, None, D_MODEL, D_FF), lambda i: (layer, which, 0, 0), pipeline_mode=resident),
            pl.BlockSpec((None, None, D_MODEL, D_FF), lambda i: (layer, which, 0, 0), pipeline_mode=resident),
            pl.BlockSpec((None, None, D_FF, D_MODEL), lambda i: (layer, which, 0, 0), pipeline_mode=resident),
            *tail_in_specs,
        ],
        out_specs=(pl.BlockSpec((tm, D_MODEL), lambda i: (i, 0)), *tail_out_specs),
        compiler_params=pltpu.CompilerParams(
            dimension_semantics=("parallel",),
            vmem_limit_bytes=_vmem_limit(blocks, weights)),
        name=name,
    )(x, gains, gains, wg, wu, wd, *operands)
    return out if tail is not None else out[0]


def _glu_stage(x, g_ref, w_ref, b_ref, o_ref):
    hn = _rms(x, g_ref[...]).astype(BF16)
    for j in range(D_MODEL // GLU_COLS):
        val = slice(j * GLU_COLS, (j + 1) * GLU_COLS)
        gate = slice(D_MODEL + j * GLU_COLS, D_MODEL + (j + 1) * GLU_COLS)
        a = jnp.dot(hn, w_ref[:, val], preferred_element_type=F32) + b_ref[:, val]
        g = jnp.dot(hn, w_ref[:, gate], preferred_element_type=F32) + b_ref[:, gate]
        o_ref[:, val] = a * _sigmoid(g)


def _glu_tail(t, gains, k_gain, w_pw1, b_pw1, j):
    tm = FFN_TM
    w_spec = pl.BlockSpec((None, D_MODEL, 2 * D_MODEL), lambda i: (j, 0, 0), pipeline_mode=pl.Buffered(1))
    return (_glu_stage, (gains, w_pw1, b_pw1),
            [_row_spec(k_gain), w_spec, _row_spec(j, 2 * D_MODEL)],
            (jax.ShapeDtypeStruct((t, D_MODEL), F32),),
            (pl.BlockSpec((tm, D_MODEL), lambda i: (i, 0)),),
            _nbytes((D_MODEL, 2 * D_MODEL), BF16),
            _nbytes((tm, D_MODEL), F32) + 3 * _nbytes((1, D_MODEL), F32),
            "ffn_glu")


def _convmix_body(x_ref, h_ref, hprev_ref, hnext_ref, wdw_ref, bdw_ref, lng_ref, lnb_ref,
                  w2_ref, b2_ref, g_ref, o_ref, hp_ref, c_ref):
    i = pl.program_id(1)
    tm = h_ref.shape[1]
    pieces = ((0, jnp.where(i > 0, hprev_ref[0], 0.0)),
              (HALO_ROWS, h_ref[0]),
              (HALO_ROWS + tm, jnp.where(i < pl.num_programs(1) - 1, hnext_ref[0], 0.0)))
    for row0, piece in pieces:
        for c in range(D_MODEL // LANES):
            hp_ref[c // 2, pl.ds(2 * row0 + c % 2, piece.shape[0], stride=2), :] = piece[:, c * LANES:(c + 1) * LANES]

    base = HALO_ROWS - CONV_HALF
    n_tiles = -(-(base + CONV_KERNEL) // SUBLANES)
    win = CONV_ROWS + (n_tiles - 1) * SUBLANES
    for c in range(D_MODEL // LANES):
        cols = slice(c * LANES, (c + 1) * LANES)

        def rows(r, carry, c=c, cols=cols):
            r0 = pl.multiple_of(r * CONV_ROWS, CONV_ROWS)
            acc = None
            for shift in range(SUBLANES):
                vs = hp_ref[c // 2, pl.ds(2 * (r0 + shift) + c % 2, win, stride=2), :]
                for tile in range(n_tiles):
                    k = tile * SUBLANES + shift - base
                    if 0 <= k < CONV_KERNEL:
                        term = vs[tile * SUBLANES:tile * SUBLANES + CONV_ROWS] * wdw_ref[k:k + 1, cols]
                        acc = term if acc is None else acc + term
            c_ref[pl.ds(r0, CONV_ROWS), cols] = acc
            return carry

        lax.fori_loop(0, tm // CONV_ROWS, rows, 0, unroll=CONV_UNROLL)

    y = c_ref[...] + bdw_ref[...]
    mu = jnp.mean(y, axis=-1, keepdims=True)
    yc = y - mu
    var = jnp.mean(yc * yc, axis=-1, keepdims=True)
    y = yc * lax.rsqrt(var + EPS) * lng_ref[...] + lnb_ref[...]
    y = (y * _sigmoid(y)).astype(BF16)
    h = jnp.dot(y, w2_ref[...], preferred_element_type=F32) + b2_ref[...]
    o_ref[0] = x_ref[0] + _rms(h, g_ref[...])


def _convmix(x, h, gains, k_gain, w_dw, b_dw, ln_g, ln_b, w_pw2, b_pw2, j):
    b, s, _ = x.shape
    tm = CONV_TM
    halo_per_tile = tm // HALO_ROWS
    n_halo = s // HALO_ROWS
    blocks = (3 * _nbytes((tm, D_MODEL), F32) + 2 * _nbytes((HALO_ROWS, D_MODEL), F32)
              + _nbytes((CONV_KERNEL, D_MODEL), F32) + _nbytes((D_MODEL, D_MODEL), BF16)
              + 6 * _nbytes((1, D_MODEL), F32))
    scratch = _nbytes((tm + 2 * HALO_ROWS, D_MODEL), F32) + _nbytes((tm, D_MODEL), F32)
    return pl.pallas_call(
        _convmix_body,
        out_shape=jax.ShapeDtypeStruct((b, s, D_MODEL), F32),
        grid=(b, s // tm),
        in_specs=[
            pl.BlockSpec((1, tm, D_MODEL), lambda bi, i: (bi, i, 0)),
            pl.BlockSpec((1, tm, D_MODEL), lambda bi, i: (bi, i, 0)),
            pl.BlockSpec((1, HALO_ROWS, D_MODEL),
                         lambda bi, i: (bi, jnp.maximum(i * halo_per_tile - 1, 0), 0)),
            pl.BlockSpec((1, HALO_ROWS, D_MODEL),
                         lambda bi, i: (bi, jnp.minimum((i + 1) * halo_per_tile, n_halo - 1), 0)),
            _mat_spec(j, CONV_KERNEL, D_MODEL),
            _row_spec(j), _row_spec(j), _row_spec(j),
            _mat_spec(j, D_MODEL, D_MODEL),
            _row_spec(j),
            _row_spec(k_gain),
        ],
        out_specs=pl.BlockSpec((1, tm, D_MODEL), lambda bi, i: (bi, i, 0)),
        scratch_shapes=[pltpu.VMEM((D_MODEL // (2 * LANES), 2 * (tm + 2 * HALO_ROWS), LANES), F32),
                        pltpu.VMEM((tm, D_MODEL), F32)],
        compiler_params=pltpu.CompilerParams(
            dimension_semantics=("parallel", "arbitrary"),
            vmem_limit_bytes=_vmem_limit(blocks, scratch)),
        name="convmix",
    )(x, h, h, h, w_dw, b_dw, ln_g, ln_b, w_pw2, b_pw2, gains)


def _rope_tables(s):
    half = HEAD_DIM // 2
    inv = ROPE_THETA ** (-jnp.arange(half, dtype=F32) / half)
    ang = jnp.arange(s).astype(F32)[:, None] * inv[None, :]
    cos, sin, zero = jnp.cos(ang), jnp.sin(ang), jnp.zeros_like(ang)
    reps = LANES // HEAD_DIM
    cos_row = jnp.tile(jnp.concatenate([cos, cos], axis=-1), (1, reps))
    sin_lo = jnp.tile(jnp.concatenate([-sin, zero], axis=-1), (1, reps))
    sin_hi = jnp.tile(jnp.concatenate([zero, sin], axis=-1), (1, reps))
    return cos_row, sin_lo, sin_hi, cos.T, sin.T


def _qkv_stage(x, g_ref, w_ref, cos_ref, slo_ref, shi_ref, cost_ref, sint_ref, qt_ref, kp_ref, vt_ref):
    tm = x.shape[0]
    hn = _rms(x, g_ref[...]).astype(BF16)
    kv = jnp.dot(hn, w_ref[:, Q_DIM:], preferred_element_type=F32)
    q = jnp.dot(hn, w_ref[:, :Q_DIM], preferred_element_type=F32)
    half = HEAD_DIM // 2
    heads_per_group = LANES // HEAD_DIM

    for c in range(Q_DIM // LANES):
        qc = q[:, c * LANES:(c + 1) * LANES]
        for n in range(tm // BLOCK):
            qct = qc[n * BLOCK:(n + 1) * BLOCK].T
            cos = cost_ref[:, n * BLOCK:(n + 1) * BLOCK]
            sin = sint_ref[:, n * BLOCK:(n + 1) * BLOCK]
            for hl in range(heads_per_group):
                head = c * heads_per_group + hl
                kvh, hg = head // GROUP, head % GROUP
                x1 = qct[hl * HEAD_DIM:hl * HEAD_DIM + half]
                x2 = qct[hl * HEAD_DIM + half:(hl + 1) * HEAD_DIM]
                roped = jnp.concatenate([x1 * cos - x2 * sin, x2 * cos + x1 * sin], axis=0)
                qt_ref[n, kvh, :, hg * BLOCK:(hg + 1) * BLOCK] = roped.astype(BF16)

    cos, slo, shi = cos_ref[...], slo_ref[...], shi_ref[...]
    for c in range(KV_DIM // LANES):
        kc = kv[:, c * LANES:(c + 1) * LANES]
        kc = kc * cos + pltpu.roll(kc, LANES - half, 1) * slo + pltpu.roll(kc, half, 1) * shi
        kp_ref[2 * c] = kc.astype(BF16)
        kp_ref[2 * c + 1] = pltpu.roll(kc, HEAD_DIM, 1).astype(BF16)
        vc = kv[:, KV_DIM + c * LANES:KV_DIM + (c + 1) * LANES]
        for n in range(tm // BLOCK):
            vt_ref[c * LANES:(c + 1) * LANES, n * BLOCK:(n + 1) * BLOCK] = (
                vc[n * BLOCK:(n + 1) * BLOCK].T.astype(BF16))


def _qkv_tail(gains, k_gain, w_qkv, j, tables, b, s):
    t = b * s
    tm = FFN_TM
    tiles_per_seq = s // tm
    half = HEAD_DIM // 2
    tab_row = pl.BlockSpec((tm, LANES), lambda i: (i % tiles_per_seq, 0))
    tab_col = pl.BlockSpec((half, tm), lambda i: (0, i % tiles_per_seq))
    w_spec = pl.BlockSpec((None, D_MODEL, QKV_DIM), lambda i: (j, 0, 0), pipeline_mode=pl.Buffered(1))
    return (_qkv_stage, (gains, w_qkv, *tables),
            [_row_spec(k_gain), w_spec, tab_row, tab_row, tab_row, tab_col, tab_col],
            (jax.ShapeDtypeStruct((t // BLOCK, N_KV_HEADS, HEAD_DIM, GROUP * BLOCK), BF16),
             jax.ShapeDtypeStruct((b, N_KV_HEADS, s, LANES), BF16),
             jax.ShapeDtypeStruct((b, KV_DIM, s), BF16)),
            (pl.BlockSpec((tm // BLOCK, N_KV_HEADS, HEAD_DIM, GROUP * BLOCK), lambda i: (i, 0, 0, 0)),
             pl.BlockSpec((None, N_KV_HEADS, tm, LANES), lambda i: (i // tiles_per_seq, 0, i % tiles_per_seq, 0)),
             pl.BlockSpec((None, KV_DIM, tm), lambda i: (i // tiles_per_seq, 0, i % tiles_per_seq))),
            _nbytes((D_MODEL, QKV_DIM), BF16),
            (3 * _nbytes((tm, LANES), F32) + 2 * _nbytes((half, tm), F32)
             + _nbytes((tm, Q_DIM + K_PAD_DIM + KV_DIM), BF16) + _nbytes((1, D_MODEL), F32)),
            "ffn_qkv")


def _attn_body(sink_ref, x_ref, qt_ref, kprev_ref, k_ref, knext_ref, vprev_ref, v_ref, vnext_ref, wo_ref, g_ref,
               o_ref, kx_ref, vw_ref, oh_ref, ta_ref, tb_ref, *, layer):
    i = pl.program_id(1)
    tq = x_ref.shape[1]
    blocks_per_tile = tq // BLOCK
    n_chains = blocks_per_tile * N_KV_HEADS
    n_blocks = pl.num_programs(1) * blocks_per_tile
    kx_ref[:, 0:BLOCK] = kprev_ref[0]
    kx_ref[:, BLOCK:BLOCK + tq] = k_ref[0]
    kx_ref[:, BLOCK + tq:] = knext_ref[0]
    vx = jnp.concatenate([vprev_ref[0], v_ref[0], vnext_ref[0]], axis=1)
    for n in range(blocks_per_tile):
        vw_ref[n] = vx[:, n * BLOCK:(n + 3) * BLOCK]

    log2e = math.log2(math.e)
    scale = HEAD_DIM ** -0.5 * log2e
    key = lax.broadcasted_iota(jnp.int32, (BLOCK, BLOCK), 0)
    qry = lax.broadcasted_iota(jnp.int32, (BLOCK, BLOCK), 1)

    def scores(c, t_ref):
        n, kv = c // N_KV_HEADS, c % N_KV_HEADS
        kb = kx_ref[kv, pl.ds(pl.multiple_of(n * BLOCK, BLOCK), 3 * BLOCK), 0:HEAD_DIM]
        t_ref[...] = jnp.dot(kb, qt_ref[n, kv], preferred_element_type=F32) * scale

    def finish(c, t_ref):
        n, kv = c // N_KV_HEADS, c % N_KV_HEADS
        gb = i * blocks_per_tile + n
        off_prev = jnp.where(gb == 0, BLOCK, BLOCK - WINDOW)
        off_next = jnp.where(gb == n_blocks - 1, BLOCK, BLOCK - WINDOW)
        mask_prev = jnp.concatenate([key >= qry + off_prev] * GROUP, axis=1)
        mask_next = jnp.concatenate([key <= qry - off_next] * GROUP, axis=1)
        t_prev = jnp.where(mask_prev, t_ref[0:BLOCK], NEG_INF)
        t_own = t_ref[BLOCK:2 * BLOCK]
        t_next = jnp.where(mask_next, t_ref[2 * BLOCK:], NEG_INF)
        sink = jnp.concatenate(
            [jnp.full((1, BLOCK), sink_ref[layer, kv * GROUP + hg] * log2e, F32) for hg in range(GROUP)], axis=1)
        m = jnp.maximum(jnp.maximum(t_prev, t_own), t_next)
        m = jnp.maximum(jnp.max(m, axis=0, keepdims=True), sink)
        e_prev, e_own, e_next = jnp.exp2(t_prev - m), jnp.exp2(t_own - m), jnp.exp2(t_next - m)
        denom = jnp.sum(e_prev + e_own + e_next, axis=0, keepdims=True) + jnp.exp2(sink - m)
        p = jnp.concatenate([e_prev, e_own, e_next], axis=0).astype(BF16)
        vb = vw_ref[n, pl.ds(pl.multiple_of(kv * HEAD_DIM, HEAD_DIM), HEAD_DIM), :]
        ot = jnp.dot(vb, p, preferred_element_type=F32) * (1.0 / denom)
        r0 = pl.multiple_of(n * BLOCK, BLOCK)
        for pair in range(GROUP // 2):
            two = jnp.concatenate([ot[:, (2 * pair) * BLOCK:(2 * pair + 1) * BLOCK],
                                   ot[:, (2 * pair + 1) * BLOCK:(2 * pair + 2) * BLOCK]], axis=0)
            oh_ref[kv * (GROUP // 2) + pair, pl.ds(r0, BLOCK), :] = two.T.astype(BF16)

    t_bufs = (ta_ref, tb_ref)
    scores(0, t_bufs[0])

    def step(j, carry):
        first = ATT_UNROLL * j
        for u in range(ATT_UNROLL):
            scores(jnp.minimum(first + u + 1, n_chains - 1), t_bufs[(u + 1) % 2])
            finish(first + u, t_bufs[u % 2])
        return carry

    lax.fori_loop(0, n_chains // ATT_UNROLL, step, 0)
    oh = jnp.concatenate([oh_ref[pp] for pp in range(N_HEADS // 2)], axis=1)
    att = jnp.dot(oh, wo_ref[...], preferred_element_type=F32)
    o_ref[0] = x_ref[0] + _rms(att, g_ref[...])


def _attn(x, qt, kp, vt, gains, k_gain, w_o, sink, j):
    b, s, _ = x.shape
    tq = ATT_TQ
    blocks_per_tile = tq // BLOCK
    tiles_per_seq = s // tq
    n_blocks = s // BLOCK
    main = lambda bi, i: (bi, i, 0)
    prev_blk = lambda i: jnp.maximum(i * blocks_per_tile - 1, 0)
    next_blk = lambda i: jnp.minimum((i + 1) * blocks_per_tile, n_blocks - 1)
    kx_shape = (N_KV_HEADS, tq + 2 * BLOCK, LANES)
    vw_shape = (blocks_per_tile, KV_DIM, 3 * BLOCK)
    oh_shape = (N_HEADS // 2, tq, LANES)
    t_shape = (3 * BLOCK, GROUP * BLOCK)
    blocks = (2 * _nbytes((tq, D_MODEL), F32) + _nbytes((tq, Q_DIM), BF16) + _nbytes(kx_shape, BF16)
              + _nbytes((KV_DIM, tq + 2 * BLOCK), BF16) + _nbytes((Q_DIM, D_MODEL), BF16) + _nbytes((1, D_MODEL), F32))
    scratch = (_nbytes(kx_shape, BF16) + _nbytes(vw_shape, BF16) + _nbytes(oh_shape, BF16)
               + 2 * _nbytes(t_shape, F32))
    return pl.pallas_call(
        functools.partial(_attn_body, layer=j),
        out_shape=jax.ShapeDtypeStruct((b, s, D_MODEL), F32),
        grid=(b, tiles_per_seq),
        in_specs=[
            pl.BlockSpec(memory_space=pltpu.SMEM),
            pl.BlockSpec((1, tq, D_MODEL), main),
            pl.BlockSpec((blocks_per_tile, N_KV_HEADS, HEAD_DIM, GROUP * BLOCK),
                         lambda bi, i: (bi * tiles_per_seq + i, 0, 0, 0)),
            pl.BlockSpec((1, N_KV_HEADS, BLOCK, LANES), lambda bi, i: (bi, 0, prev_blk(i), 0)),
            pl.BlockSpec((1, N_KV_HEADS, tq, LANES), lambda bi, i: (bi, 0, i, 0)),
            pl.BlockSpec((1, N_KV_HEADS, BLOCK, LANES), lambda bi, i: (bi, 0, next_blk(i), 0)),
            pl.BlockSpec((1, KV_DIM, BLOCK), lambda bi, i: (bi, 0, prev_blk(i))),
            pl.BlockSpec((1, KV_DIM, tq), lambda bi, i: (bi, 0, i)),
            pl.BlockSpec((1, KV_DIM, BLOCK), lambda bi, i: (bi, 0, next_blk(i))),
            _mat_spec(j, Q_DIM, D_MODEL),
            _row_spec(k_gain),
        ],
        out_specs=pl.BlockSpec((1, tq, D_MODEL), main),
        scratch_shapes=[pltpu.VMEM(kx_shape, BF16), pltpu.VMEM(vw_shape, BF16), pltpu.VMEM(oh_shape, BF16),
                        pltpu.VMEM(t_shape, F32), pltpu.VMEM(t_shape, F32)],
        compiler_params=pltpu.CompilerParams(
            dimension_semantics=("parallel", "arbitrary"),
            vmem_limit_bytes=_vmem_limit(blocks, scratch)),
        name="attn",
    )(sink, x, qt, kp, kp, kp, vt, vt, vt, w_o, gains)


def _trunk(x, p):
    b, s, _ = x.shape
    t = b * s
    gains = p["gains"]
    x = x.reshape(t, D_MODEL)

    def ffn(x, layer, which, tail=None):
        k_pre = layer * N_SUBLAYER_NORMS + 4 * which
        return _ffn(x, gains, k_pre, p["wg"], p["wu"], p["wd"], layer, which, tail)

    x, h = ffn(x, 0, 0, _glu_tail(t, gains, 2, p["w_pw1"], p["b_pw1"], 0))
    x = _convmix(x.reshape(b, s, D_MODEL), h.reshape(b, s, D_MODEL), gains, 3, p["w_dw"], p["b_dw"], p["ln_g"],
                 p["ln_b"], p["w_pw2"], p["b_pw2"], 0).reshape(t, D_MODEL)
    x = ffn(x, 0, 1)

    x, qt, kp, vt = ffn(x, 1, 0, _qkv_tail(gains, N_SUBLAYER_NORMS + 2, p["w_qkv"], 0, p["rope"], b, s))
    x = _attn(x.reshape(b, s, D_MODEL), qt, kp, vt, gains, N_SUBLAYER_NORMS + 3, p["w_o"], p["sink"],
              0).reshape(t, D_MODEL)
    x = ffn(x, 1, 1)
    return x.reshape(b, s, D_MODEL)


def kernel(x_prompt, x_sample, norm_g, ffn_w_gate, ffn_w_up, ffn_w_down, conv_w_pw1, conv_b_pw1, conv_w_dw,
           conv_b_dw, conv_ln_g, conv_ln_b, conv_w_pw2, conv_b_pw2, attn_w_qkv, attn_w_o, attn_sink):
    rows = lambda a: a.reshape(-1, 1, a.shape[-1])
    p = {
        "gains": rows(norm_g),
        "wg": ffn_w_gate.astype(BF16), "wu": ffn_w_up.astype(BF16), "wd": ffn_w_down.astype(BF16),
        "w_pw1": conv_w_pw1.astype(BF16), "b_pw1": rows(conv_b_pw1),
        "w_dw": conv_w_dw, "b_dw": rows(conv_b_dw), "ln_g": rows(conv_ln_g), "ln_b": rows(conv_ln_b),
        "w_pw2": conv_w_pw2.astype(BF16), "b_pw2": rows(conv_b_pw2),
        "w_qkv": attn_w_qkv.astype(BF16), "w_o": attn_w_o.astype(BF16), "sink": attn_sink,
        "rope": _rope_tables(max(x_prompt.shape[1], x_sample.shape[1])),
    }
    return _trunk(x_prompt, p), _trunk(x_sample, p)
```
